```python
import jax
import jax.numpy as jnp
from jax import lax
import numpy as np

D_MODEL = 2048
BATCH = 2
SEQ = 8192
DEPTH = 4

GRID_W = 64
CTX_LEN = 256
N_MIXERS = 3
N_MOD = 6
D_FF = ((8 * D_MODEL + 3 * 256 - 1) // (3 * 256)) * 256
NORM_EPS = 1e-6

LRU_WIDTH = D_MODEL
LRU_BLOCKS = 16
LRU_BLOCK = LRU_WIDTH // LRU_BLOCKS
LRU_CONV = 4
LRU_C = 8.0

GLA_HEADS = 4
GLA_DK = D_MODEL // 2
GLA_DV = D_MODEL
GLA_HK = GLA_DK // GLA_HEADS
GLA_HV = GLA_DV // GLA_HEADS
GLA_RANK = 16
GLA_TAU = 16.0
GLA_CHUNK = 64

RWKV_N = 64
RWKV_H = D_MODEL // RWKV_N
RWKV_MIX = 6
RWKV_DECAY_LORA = max(32, int(round(1.8 * D_MODEL ** 0.5 / 32)) * 32)
RWKV_AAA_LORA = max(32, int(round(1.8 * D_MODEL ** 0.5 / 32)) * 32)
RWKV_GATE_LORA = max(32, int(round(0.6 * D_MODEL ** 0.8 / 32)) * 32)
RWKV_GN_EPS = 64e-5

kernel_name = 'hybrid_rglru_gla_rwkv7_flow_trunk'


def rmsnorm(x, g):
    xf = x.astype(jnp.float32)
    y = xf * lax.rsqrt(jnp.mean(xf * xf, axis=-1, keepdims=True) + NORM_EPS)
    return (y * g.astype(jnp.float32)).astype(x.dtype)


def modulate(u, shift, scale):
    return u * (1.0 + scale) + shift


def to_col_major(u, rows):
    b, n, d = u.shape
    return u.reshape(b, rows, GRID_W, d).transpose(0, 2, 1, 3).reshape(b, n, d)


def to_row_major(u, rows):
    b, n, d = u.shape
    return u.reshape(b, GRID_W, rows, d).transpose(0, 2, 1, 3).reshape(b, n, d)


def flip_seq(ts, axis):
    return tuple(jnp.flip(t, axis) for t in ts)


def swiglu(u, w_in, w_out):
    gate, up = jnp.split(u @ w_in, 2, axis=-1)
    return (jax.nn.silu(gate) * up) @ w_out


def depthwise_conv_centred(u, w, b):
    k = w.shape[0]
    left = k // 2
    y = lax.conv_general_dilated(
        u, w[:, None, :].astype(u.dtype), window_strides=(1,),
        padding=[(left, k - 1 - left)], dimension_numbers=('NWC', 'WIO', 'NWC'),
        feature_group_count=u.shape[-1])
    return y + b


def split_heads(t, n_heads):
    b, n, f = t.shape
    return t.reshape(b, n, n_heads, f // n_heads).transpose(0, 2, 1, 3).astype(jnp.float32)


def merge_heads(t):
    b, h, n, dh = t.shape
    return t.transpose(0, 2, 1, 3).reshape(b, n, h * dh)


def linear_scan(a, b, h0):
    b = b.at[:, 0].add(a[:, 0] * h0)

    def combine(e1, e2):
        a1, b1 = e1
        a2, b2 = e2
        return a1 * a2, a2 * b1 + b2

    _, h = lax.associative_scan(combine, (a, b), axis=1)
    return h, h[:, -1]


def lru_coeffs(xb, gate_w, gate_b, log_lambda):
    b, n, w = xb.shape
    xf = xb.astype(jnp.float32)
    xblk = xf.reshape(b, n, LRU_BLOCKS, LRU_BLOCK)
    gates = jnp.einsum('blnc,gncd->gblnd', xblk, gate_w.astype(jnp.float32)).reshape(2, b, n, w)
    gates = gates + gate_b.astype(jnp.float32)[:, None, None, :]
    r = jax.nn.sigmoid(gates[0])
    i = jax.nn.sigmoid(gates[1])
    log_a = -LRU_C * r * jax.nn.softplus(-log_lambda.astype(jnp.float32))
    a = jnp.exp(log_a)
    bt = jnp.sqrt(-jnp.expm1(2.0 * log_a)) * (i * xf)
    return a, bt


def rglru_mixer(u_ctx, u_lat, w_in, b_in, conv_w, conv_b, gate_w, gate_b, log_lambda,
                w_out, b_out, ctx_out):
    def project(u, with_gate):
        if with_gate:
            z = u @ w_in + b_in
            gate = jax.nn.gelu(z[..., :LRU_WIDTH], approximate=True)
            rec = z[..., LRU_WIDTH:]
        else:
            gate = None
            rec = u @ w_in[:, LRU_WIDTH:] + b_in[LRU_WIDTH:]
        return gate, depthwise_conv_centred(rec, conv_w, conv_b)

    gate_c, x_c = project(u_ctx, ctx_out)
    gate_l, x_l = project(u_lat, True)
    h_c, h_l = [], []
    for d in range(2):
        coef_c = lru_coeffs(x_c, gate_w[d], gate_b[d], log_lambda[d])
        coef_l = lru_coeffs(x_l, gate_w[d], gate_b[d], log_lambda[d])
        if d == 1:
            coef_c, coef_l = flip_seq(coef_c, 1), flip_seq(coef_l, 1)
        hc, hc_last = linear_scan(coef_c[0], coef_c[1], jnp.zeros_like(coef_c[0][:, 0]))
        hl, _ = linear_scan(coef_l[0], coef_l[1], hc_last)
        h_l.append(jnp.flip(hl, 1) if d == 1 else hl)
        if ctx_out:
            h_c.append(jnp.flip(hc, 1) if d == 1 else hc)
    y_lat = ((h_l[0] + h_l[1]).astype(u_lat.dtype) * gate_l) @ w_out + b_out
    y_ctx = None
    if ctx_out:
        y_ctx = ((h_c[0] + h_c[1]).astype(u_ctx.dtype) * gate_c) @ w_out + b_out
    return y_ctx, y_lat


def gla_chunked(q, k, v, g, s0):
    b, h, n, dk = q.shape
    dv = v.shape[-1]
    nc = n // GLA_CHUNK
    mask = jnp.tril(jnp.ones((GLA_CHUNK, GLA_CHUNK), dtype=bool))

    def chunks(t):
        return jnp.moveaxis(t.reshape(b, h, nc, GLA_CHUNK, t.shape[-1]), 2, 0)

    def step(s, xs):
        qc, kc, vc, gc = xs
        bcum = jnp.cumsum(gc, axis=-2)
        q_e = qc * jnp.exp(bcum)
        k_e = kc * jnp.exp(-bcum)
        att = jnp.where(mask, jnp.einsum('bhik,bhjk->bhij', q_e, k_e), 0.0)
        o = jnp.einsum('bhij,bhjv->bhiv', att, vc) + jnp.einsum('bhik,bhkv->bhiv', q_e, s)
        b_last = bcum[:, :, -1, :]
        k_rem = kc * jnp.exp(b_last[:, :, None, :] - bcum)
        s = jnp.exp(b_last)[..., None] * s + jnp.einsum('bhjk,bhjv->bhkv', k_rem, vc)
        return s, o

    s_last, o = lax.scan(step, s0, (chunks(q), chunks(k), chunks(v), chunks(g)))
    return jnp.moveaxis(o, 0, 2).reshape(b, h, n, dv), s_last


def gla_mixer(u_ctx, u_lat, w_in, b_r, gate_w1, gate_w2, gate_b, norm_g, w_out, ctx_out):
    def project(u):
        q, k, v, r = jnp.split(u @ w_in, [GLA_DK, 2 * GLA_DK, 2 * GLA_DK + GLA_DV], axis=-1)
        g = [split_heads(jax.nn.log_sigmoid(
                ((u @ gate_w1[d]) @ gate_w2[d] + gate_b[d]).astype(jnp.float32)) / GLA_TAU,
                GLA_HEADS) for d in range(2)]
        qkv = (split_heads(q, GLA_HEADS) * GLA_HK ** -0.5,
               split_heads(k, GLA_HEADS), split_heads(v, GLA_HEADS))
        return qkv, g, r

    qkv_c, g_c, r_c = project(u_ctx)
    qkv_l, g_l, r_l = project(u_lat)
    s0 = jnp.zeros((u_lat.shape[0], GLA_HEADS, GLA_HK, GLA_HV), jnp.float32)
    o_c, o_l = [], []
    for d in range(2):
        tc = (qkv_c[0], qkv_c[1], qkv_c[2], g_c[d])
        tl = (qkv_l[0], qkv_l[1], qkv_l[2], g_l[d])
        if d == 1:
            tc, tl = flip_seq(tc, 2), flip_seq(tl, 2)
        oc, sc = gla_chunked(tc[0], tc[1], tc[2], tc[3], s0)
        ol, _ = gla_chunked(tl[0], tl[1], tl[2], tl[3], sc)
        o_l.append(jnp.flip(ol, 2) if d == 1 else ol)
        if ctx_out:
            o_c.append(jnp.flip(oc, 2) if d == 1 else oc)

    def readout(o, r):
        o = o * lax.rsqrt(jnp.mean(o * o, axis=-1, keepdims=True) + NORM_EPS)
        o = merge_heads(o).astype(r.dtype) * norm_g
        return (o * jax.nn.silu(r + b_r)) @ w_out

    y_lat = readout(o_l[0] + o_l[1], r_l)
    y_ctx = readout(o_c[0] + o_c[1], r_c) if ctx_out else None
    return y_ctx, y_lat


def centred_shift(u):
    up = jnp.pad(u, ((0, 0), (1, 1), (0, 0)))
    return 0.5 * (up[:, :-2] + up[:, 2:])


def rwkv_scan(r, w, k, v, aa, bb, s0):
    def step(s, xs):
        r_t, w_t, k_t, v_t, a_t, b_t = xs
        sa = jnp.einsum('bhvk,bhk->bhv', s, a_t)
        s = s * w_t[:, :, None, :] + sa[..., None] * b_t[:, :, None, :] + v_t[..., None] * k_t[:, :, None, :]
        return s, jnp.einsum('bhvk,bhk->bhv', s, r_t)

    xs = tuple(jnp.moveaxis(t, 1, 0) for t in (r, w, k, v, aa, bb))
    s_last, y = lax.scan(step, s0, xs)
    return jnp.moveaxis(y, 0, 1), s_last


def rwkv7_mixer(u_ctx, u_lat, mu, w_rkv, w0, w1, w2, a0, a1, a2, g1, g2, k_k, k_a, r_k,
                ln_w, ln_b, w_out, ctx_out):
    kk_scale = k_k.astype(jnp.float32).reshape(RWKV_H, RWKV_N)
    ka_scale = k_a.astype(jnp.float32).reshape(RWKV_H, RWKV_N)
    rk = r_k.astype(jnp.float32)

    def prep(u):
        b, n, _ = u.shape
        dx = centred_shift(u) - u
        xr, xw, xk, xv, xa, xg = [u + dx * mu[m] for m in range(RWKV_MIX)]

        def heads(t):
            return t.astype(jnp.float32).reshape(b, n, RWKV_H, RWKV_N)

        r = heads(xr @ w_rkv[0])
        k = heads(xk @ w_rkv[1])
        v = heads(xv @ w_rkv[2])
        kk = k * kk_scale
        kk = kk * lax.rsqrt(jnp.maximum(jnp.sum(kk * kk, axis=-1, keepdims=True), 1e-24))
        dirs = []
        for d in range(2):
            w_raw = -jax.nn.softplus(-(w0[d] + jnp.tanh(xw @ w1[d]) @ w2[d]).astype(jnp.float32)) - 0.5
            decay = heads(jnp.exp(-jnp.exp(w_raw)))
            a = heads(jax.nn.sigmoid((a0[d] + (xa @ a1[d]) @ a2[d]).astype(jnp.float32)))
            k_d = k * (1.0 + (a - 1.0) * ka_scale)
            dirs.append((decay, k_d, a))
        return r, v, kk, dirs, xg

    r_c, v_c, kk_c, dirs_c, xg_c = prep(u_ctx)
    r_l, v_l, kk_l, dirs_l, xg_l = prep(u_lat)
    s0 = jnp.zeros((u_lat.shape[0], RWKV_H, RWKV_N, RWKV_N), jnp.float32)
    y_c, y_l = [], []
    for d in range(2):
        dc, kc_d, ac = dirs_c[d]
        dl, kl_d, al = dirs_l[d]
        tc = (r_c, dc, kc_d, v_c, -kk_c, kk_c * ac)
        tl = (r_l, dl, kl_d, v_l, -kk_l, kk_l * al)
        if d == 1:
            tc, tl = flip_seq(tc, 1), flip_seq(tl, 1)
        oc, sc = rwkv_scan(tc[0], tc[1], tc[2], tc[3], tc[4], tc[5], s0)
        ol, _ = rwkv_scan(tl[0], tl[1], tl[2], tl[3], tl[4], tl[5], sc)
        y_l.append(jnp.flip(ol, 1) if d == 1 else ol)
        if ctx_out:
            y_c.append(jnp.flip(oc, 1) if d == 1 else oc)

    def readout(ys, r, v, dirs, xg):
        b, n, _, _ = r.shape
        y = ys[0] + ys[1]
        mean = jnp.mean(y, axis=-1, keepdims=True)
        var = jnp.mean(jnp.square(y - mean), axis=-1, keepdims=True)
        y = ((y - mean) * lax.rsqrt(var + RWKV_GN_EPS)).reshape(b, n, D_MODEL)
        y = y * ln_w.astype(jnp.float32) + ln_b.astype(jnp.float32)
        bonus = (jnp.sum(r * dirs[0][1] * rk, axis=-1, keepdims=True)
                 + jnp.sum(r * dirs[1][1] * rk, axis=-1, keepdims=True)) * v
        y = (y + bonus.reshape(b, n, D_MODEL)).astype(xg.dtype)
        g = jax.nn.sigmoid(xg @ g1) @ g2
        return (y * g) @ w_out

    y_lat = readout(y_l, r_l, v_l, dirs_l, xg_l)
    y_ctx = readout(y_c, r_c, v_c, dirs_c, xg_c) if ctx_out else None
    return y_ctx, y_lat


def ffn_sublayer(h, mods, norm_g, w_in, w_out):
    u = modulate(rmsnorm(h, norm_g[2]), mods[3], mods[4])
    return h + mods[5] * rmsnorm(swiglu(u, w_in, w_out), norm_g[3])


def setup_inputs(seed: int = 0) -> dict:
    key = jax.random.key(seed)
    keys = iter(jax.random.split(key, 48))
    f32 = jnp.float32
    D = D_MODEL
    n_lru = len(range(0, DEPTH, N_MIXERS))
    n_gla = len(range(1, DEPTH, N_MIXERS))
    n_rwkv = len(range(2, DEPTH, N_MIXERS))

    def nrm(shape, scale):
        return scale * jax.random.normal(next(keys), shape, f32)

    def unif(shape, lo, hi):
        return jax.random.uniform(next(keys), shape, f32, lo, hi)

    lam = unif((n_lru, 2, LRU_WIDTH), 0.9, 0.999) ** (1.0 / LRU_C)
    return {
        'x': nrm((BATCH, SEQ, D), 1.0),
        'c': nrm((BATCH, D), 1.0),
        'ctx': nrm((BATCH, CTX_LEN, D), 1.0),
        'c_ctx': nrm((D,), 1.0),
        'ada_w': nrm((DEPTH, D, N_MOD * D), 0.5 * D ** -0.5),
        'ada_b': nrm((DEPTH, N_MOD * D), 0.02),
        'norm_g': 1.0 + nrm((DEPTH, 4, D), 0.05),
        'ffn_w_in': nrm((DEPTH, D, 2 * D_FF), D ** -0.5),
        'ffn_w_out': nrm((DEPTH, D_FF, D), D_FF ** -0.5),
        'lru_w_in': nrm((n_lru, D, 2 * LRU_WIDTH), D ** -0.5),
        'lru_b_in': nrm((n_lru, 2 * LRU_WIDTH), 0.02),
        'lru_conv_w': nrm((n_lru, LRU_CONV, LRU_WIDTH), LRU_CONV ** -0.5),
        'lru_conv_b': nrm((n_lru, LRU_WIDTH), 0.02),
        'lru_gate_w': nrm((n_lru, 2, 2, LRU_BLOCKS, LRU_BLOCK, LRU_BLOCK), LRU_BLOCK ** -0.5),
        'lru_gate_b': nrm((n_lru, 2, 2, LRU_WIDTH), 0.1),
        'lru_log_lambda': jnp.log(lam) - jnp.log1p(-lam),
        'lru_w_out': nrm((n_lru, LRU_WIDTH, D), LRU_WIDTH ** -0.5),
        'lru_b_out': nrm((n_lru, D), 0.02),
        'gla_w_in': nrm((n_gla, D, 2 * GLA_DK + 2 * GLA_DV), D ** -0.5),
        'gla_b_r': nrm((n_gla, GLA_DV), 0.02),
        'gla_gate_w1': nrm((n_gla, 2, D, GLA_RANK), D ** -0.5),
        'gla_gate_w2': nrm((n_gla, 2, GLA_RANK, GLA_DK), GLA_RANK ** -0.5),
        'gla_gate_b': unif((n_gla, 2, GLA_DK), 0.0, 4.0),
        'gla_norm_g': 1.0 + nrm((n_gla, GLA_DV), 0.05),
        'gla_w_out': nrm((n_gla, GLA_DV, D), GLA_DV ** -0.5),
        'rwkv_mu': unif((n_rwkv, RWKV_MIX, D), 0.0, 1.0),
        'rwkv_w_rkv': nrm((n_rwkv, 3, D, D), D ** -0.5),
        'rwkv_w0': unif((n_rwkv, 2, D), -6.5, -1.5),
        'rwkv_w1': nrm((n_rwkv, 2, D, RWKV_DECAY_LORA), D ** -0.5),
        'rwkv_w2': nrm((n_rwkv, 2, RWKV_DECAY_LORA, D), 0.1 * RWKV_DECAY_LORA ** -0.5),
        'rwkv_a0': nrm((n_rwkv, 2, D), 0.1),
        'rwkv_a1': nrm((n_rwkv, 2, D, RWKV_AAA_LORA), D ** -0.5),
        'rwkv_a2': nrm((n_rwkv, 2, RWKV_AAA_LORA, D), RWKV_AAA_LORA ** -0.5),
        'rwkv_g1': nrm((n_rwkv, D, RWKV_GATE_LORA), D ** -0.5),
        'rwkv_g2': nrm((n_rwkv, RWKV_GATE_LORA, D), RWKV_GATE_LORA ** -0.5),
        'rwkv_k_k': 0.85 + nrm((n_rwkv, D), 0.02),
        'rwkv_k_a': 1.0 + nrm((n_rwkv, D), 0.02),
        'rwkv_r_k': nrm((n_rwkv, RWKV_H, RWKV_N), 0.1),
        'rwkv_ln_w': 1.0 + nrm((n_rwkv, D), 0.05),
        'rwkv_ln_b': nrm((n_rwkv, D), 0.02),
        'rwkv_w_out': nrm((n_rwkv, D, D), D ** -0.5),
    }


def reference(x, c, ctx, c_ctx, ada_w, ada_b, norm_g, ffn_w_in, ffn_w_out,
              lru_w_in, lru_b_in, lru_conv_w, lru_conv_b, lru_gate_w, lru_gate_b,
              lru_log_lambda, lru_w_out, lru_b_out,
              gla_w_in, gla_b_r, gla_gate_w1, gla_gate_w2, gla_gate_b, gla_norm_g, gla_w_out,
              rwkv_mu, rwkv_w_rkv, rwkv_w0, rwkv_w1, rwkv_w2, rwkv_a0, rwkv_a1, rwkv_a2,
              rwkv_g1, rwkv_g2, rwkv_k_k, rwkv_k_a, rwkv_r_k, rwkv_ln_w, rwkv_ln_b, rwkv_w_out):
    rows = x.shape[1] // GRID_W
    cond_lat = jax.nn.silu(c)[:, None, :]
    cond_ctx = jax.nn.silu(c_ctx)[None, None, :]
    h_lat, h_ctx = x, ctx
    for i in range(DEPTH):
        ctx_out = i < DEPTH - 1
        kind, j = i % N_MIXERS, i // N_MIXERS
        mods_lat = jnp.split(cond_lat @ ada_w[i] + ada_b[i], N_MOD, axis=-1)
        mods_ctx = jnp.split(cond_ctx @ ada_w[i] + ada_b[i], N_MOD, axis=-1)
        u_lat = modulate(rmsnorm(h_lat, norm_g[i, 0]), mods_lat[0], mods_lat[1])
        u_ctx = modulate(rmsnorm(h_ctx, norm_g[i, 0]), mods_ctx[0], mods_ctx[1])
        col_major = i % 2 == 1
        if col_major:
            u_lat = to_col_major(u_lat, rows)
        if kind == 0:
            y_ctx, y_lat = rglru_mixer(u_ctx, u_lat, lru_w_in[j], lru_b_in[j], lru_conv_w[j],
                                       lru_conv_b[j], lru_gate_w[j], lru_gate_b[j],
                                       lru_log_lambda[j], lru_w_out[j], lru_b_out[j], ctx_out)
        elif kind == 1:
            y_ctx, y_lat = gla_mixer(u_ctx, u_lat, gla_w_in[j], gla_b_r[j], gla_gate_w1[j],
                                     gla_gate_w2[j], gla_gate_b[j], gla_norm_g[j],
                                     gla_w_out[j], ctx_out)
        else:
            y_ctx, y_lat = rwkv7_mixer(u_ctx, u_lat, rwkv_mu[j], rwkv_w_rkv[j], rwkv_w0[j],
                                       rwkv_w1[j], rwkv_w2[j], rwkv_a0[j], rwkv_a1[j],
                                       rwkv_a2[j], rwkv_g1[j], rwkv_g2[j], rwkv_k_k[j],
                                       rwkv_k_a[j], rwkv_r_k[j], rwkv_ln_w[j], rwkv_ln_b[j],
                                       rwkv_w_out[j], ctx_out)
        if col_major:
            y_lat = to_row_major(y_lat, rows)
        h_lat = h_lat + mods_lat[2] * rmsnorm(y_lat, norm_g[i, 1])
        h_lat = ffn_sublayer(h_lat, mods_lat, norm_g[i], ffn_w_in[i], ffn_w_out[i])
        if ctx_out:
            h_ctx = h_ctx + mods_ctx[2] * rmsnorm(y_ctx, norm_g[i, 1])
            h_ctx = ffn_sublayer(h_ctx, mods_ctx, norm_g[i], ffn_w_in[i], ffn_w_out[i])
    return h_lat
```

```python
import functools

import jax
import jax.numpy as jnp
from jax import lax
from jax.experimental import pallas as pl
from jax.experimental.pallas import tpu as pltpu

F32 = jnp.float32
BF16 = jnp.bfloat16

GRID_W = 64
N_MIXERS = 3
N_MOD = 6
NORM_EPS = 1e-6
LRU_BLOCKS = 16
LRU_C = 8.0
GLA_HEADS = 4
GLA_TAU = 16.0
GLA_CHUNK = 64
RWKV_N = 64
RWKV_CHUNK = 64
RWKV_GN_EPS = 64e-5
LANES = 128
SUBLANES = 8
VMEM_LIMIT = 52 * 1024 * 1024
HIGHEST = lax.Precision.HIGHEST


def _cparams(*sem):
    return pltpu.CompilerParams(dimension_semantics=sem, vmem_limit_bytes=VMEM_LIMIT)


def _pick(n, cands):
    for c in cands:
        if n % c == 0:
            return c
    return n


def _dot(a, b):
    return jnp.dot(a.astype(BF16), b.astype(BF16), preferred_element_type=F32)


def _dot_nt(a, b):
    return lax.dot_general(a.astype(BF16), b.astype(BF16), (((1,), (1,)), ((), ())),
                           preferred_element_type=F32)


def _dot_tn(a, b):
    return lax.dot_general(a.astype(BF16), b.astype(BF16), (((0,), (0,)), ((), ())),
                           preferred_element_type=F32)


def _dot_f32(a, b):
    return jnp.dot(a, b, precision=HIGHEST, preferred_element_type=F32)


def _sigmoid(x):
    return 1.0 / (1.0 + jnp.exp(-x))


def _softplus(x):
    return jnp.maximum(x, 0.0) + jnp.log(1.0 + jnp.exp(-jnp.abs(x)))


def _mods_kernel(x_ref, w_ref, b_ref, o_ref):
    x = x_ref[...]
    x = x * _sigmoid(x)
    o_ref[...] = _dot(x, w_ref[...]) + b_ref[...]


def _mods(cond, ada_w, ada_b):
    n_layers, d, n = ada_w.shape
    tn = _pick(n, (1024, 512, 256, 128))
    return pl.pallas_call(
        _mods_kernel,
        grid=(n_layers, n // tn),
        in_specs=[pl.BlockSpec((8, d), lambda l, j: (0, 0)),
                  pl.BlockSpec((None, d, tn), lambda l, j: (l, 0, j)),
                  pl.BlockSpec((None, 1, tn), lambda l, j: (l, 0, j))],
        out_specs=pl.BlockSpec((None, 8, tn), lambda l, j: (l, 0, j)),
        out_shape=jax.ShapeDtypeStruct((n_layers, 8, n), F32),
        compiler_params=_cparams("parallel", "parallel"),
        name="ada_mods",
    )(cond, ada_w, ada_b.reshape(n_layers, 1, n))


def _norm_mod_kernel(*refs, k_shift, k_scale, aliased):
    if aliased:
        h_ref, g_ref, m_ref, _, o_ref = refs
    else:
        h_ref, g_ref, m_ref, o_ref = refs
    x = h_ref[...]
    y = x * lax.rsqrt(jnp.mean(x * x, axis=-1, keepdims=True) + NORM_EPS) * g_ref[...]
    y = y * (1.0 + m_ref[k_scale:k_scale + 1, :]) + m_ref[k_shift:k_shift + 1, :]
    o_ref[...] = y.astype(o_ref.dtype)


def _norm_mod(h_lat, h_ctx, g, m_lat, m_ctx, k_shift, k_scale, col_major, out_dtype):
    b, n, d = h_lat.shape
    rows = n // GRID_W
    n_ctx = 0 if h_ctx is None else h_ctx.shape[1]
    g = g.reshape(1, d)
    out_shape = jax.ShapeDtypeStruct((b, n + n_ctx, d), out_dtype)
    kern = functools.partial(_norm_mod_kernel, k_shift=k_shift, k_scale=k_scale)
    if col_major:
        tt = rows
        grid = (b, GRID_W)
        h_in = h_lat.reshape(b, rows, GRID_W * d)
        h_spec = pl.BlockSpec((None, rows, d), lambda i, j: (i, 0, j))
    else:
        tt = _pick(n, (256, 128, 64, 32, 16, 8))
        grid = (b, n // tt)
        h_in = h_lat
        h_spec = pl.BlockSpec((None, tt, d), lambda i, j: (i, j, 0))
    us = pl.pallas_call(
        functools.partial(kern, aliased=False),
        grid=grid,
        in_specs=[h_spec,
                  pl.BlockSpec((1, d), lambda i, j: (0, 0)),
                  pl.BlockSpec((None, N_MOD, d), lambda i, j: (i, 0, 0))],
        out_specs=pl.BlockSpec((None, tt, d), lambda i, j: (i, j, 0)),
        out_shape=out_shape,
        compiler_params=_cparams("parallel", "parallel"),
        name="norm_mod_lat",
    )(h_in, g, m_lat)
    if h_ctx is None:
        return us
    tc = _pick(n_ctx, (256, 128, 64, 32, 16, 8))
    assert n % tc == 0
    off = n // tc
    return pl.pallas_call(
        functools.partial(kern, aliased=True),
        grid=(b, n_ctx // tc),
        in_specs=[pl.BlockSpec((None, tc, d), lambda i, j: (i, j, 0)),
                  pl.BlockSpec((1, d), lambda i, j: (0, 0)),
                  pl.BlockSpec((None, N_MOD, d), lambda i, j: (0, 0, 0)),
                  pl.BlockSpec(memory_space=pl.ANY)],
        out_specs=pl.BlockSpec((None, tc, d), lambda i, j: (i, off + j, 0)),
        out_shape=out_shape,
        input_output_aliases={3: 0},
        compiler_params=_cparams("parallel", "parallel"),
        name="norm_mod_ctx",
    )(h_ctx, g, m_ctx.reshape(1, N_MOD, d), us)


def _resid_kernel(h_ref, y_ref, g_ref, m_ref, o_ref, *, k_gate):
    y = y_ref[...]
    yn = y * lax.rsqrt(jnp.mean(y * y, axis=-1, keepdims=True) + NORM_EPS) * g_ref[...]
    o_ref[...] = h_ref[...] + m_ref[k_gate:k_gate + 1, :] * yn


def _resid(h_lat, h_ctx, y, g, m_lat, m_ctx, k_gate, col_major):
    b, n, d = h_lat.shape
    rows = n // GRID_W
    g = g.reshape(1, d)
    kern = functools.partial(_resid_kernel, k_gate=k_gate)
    if col_major:
        tt = rows
        grid = (b, GRID_W)
        h_in = h_lat.reshape(b, rows, GRID_W * d)
        h_spec = pl.BlockSpec((None, rows, d), lambda i, j: (i, 0, j))
        out_shape = jax.ShapeDtypeStruct((b, rows, GRID_W * d), F32)
    else:
        tt = _pick(n, (256, 128, 64, 32, 16, 8))
        grid = (b, n // tt)
        h_in = h_lat
        h_spec = pl.BlockSpec((None, tt, d), lambda i, j: (i, j, 0))
        out_shape = jax.ShapeDtypeStruct((b, n, d), F32)
    new_lat = pl.pallas_call(
        kern,
        grid=grid,
        in_specs=[h_spec,
                  pl.BlockSpec((None, tt, d), lambda i, j: (i, j, 0)),
                  pl.BlockSpec((1, d), lambda i, j: (0, 0)),
                  pl.BlockSpec((None, N_MOD, d), lambda i, j: (i, 0, 0))],
        out_specs=h_spec,
        out_shape=out_shape,
        compiler_params=_cparams("parallel", "parallel"),
        name="resid_lat",
    )(h_in, y, g, m_lat).reshape(b, n, d)
    if h_ctx is None:
        return new_lat, None
    n_ctx = h_ctx.shape[1]
    tc = _pick(n_ctx, (256, 128, 64, 32, 16, 8))
    off = n // tc
    new_ctx = pl.pallas_call(
        kern,
        grid=(b, n_ctx // tc),
        in_specs=[pl.BlockSpec((None, tc, d), lambda i, j: (i, j, 0)),
                  pl.BlockSpec((None, tc, d), lambda i, j: (i, off + j, 0)),
                  pl.BlockSpec((1, d), lambda i, j: (0, 0)),
                  pl.BlockSpec((None, N_MOD, d), lambda i, j: (0, 0, 0))],
        out_specs=pl.BlockSpec((None, tc, d), lambda i, j: (i, j, 0)),
        out_shape=jax.ShapeDtypeStruct((b, n_ctx, d), F32),
        compiler_params=_cparams("parallel", "parallel"),
        name="resid_ctx",
    )(h_ctx, y, g, m_ctx.reshape(1, N_MOD, d))
    return new_lat, new_ctx


def _gelu_tanh(x):
    return 0.5 * x * (1.0 + jnp.tanh(0.7978845608028654 * (x + 0.044715 * x * x * x)))


def _apply_act(z, act):
    if act is None:
        return z
    if act == "gelu":
        return _gelu_tanh(z)
    if act == "tanh":
        return jnp.tanh(z)
    if act == "sigmoid":
        return _sigmoid(z)
    if act == "gla_gate":
        return -_softplus(-z) * (1.0 / GLA_TAU)
    if act == "rwkv_decay":
        return -jnp.exp(-_softplus(-z) - 0.5)
    raise ValueError(act)


def _mm_kernel(x_ref, w_ref, b_ref, o_ref, *, act):
    z = _dot(x_ref[...], w_ref[...]) + b_ref[...]
    o_ref[...] = _apply_act(z, act).astype(o_ref.dtype)


def _mm(x, w, bias=None, act=None, out_dtype=F32, col0=0, ncols=None):
    m, k = x.shape
    n_total = w.shape[1]
    ncols = n_total if ncols is None else ncols
    tm = _pick(m, (512, 256, 128, 64, 32, 16, 8))
    tn = _pick(ncols, (1024, 512, 256, 128)) if k <= 2048 else _pick(ncols, (512, 256, 128))
    assert col0 % tn == 0
    off = col0 // tn
    if bias is None:
        bias = jnp.zeros((ncols,), F32)
    return pl.pallas_call(
        functools.partial(_mm_kernel, act=act),
        grid=(ncols // tn, m // tm),
        in_specs=[pl.BlockSpec((tm, k), lambda j, i: (i, 0)),
                  pl.BlockSpec((k, tn), lambda j, i: (0, off + j)),
                  pl.BlockSpec((1, tn), lambda j, i: (0, j))],
        out_specs=pl.BlockSpec((tm, tn), lambda j, i: (i, j)),
        out_shape=jax.ShapeDtypeStruct((m, ncols), out_dtype),
        compiler_params=_cparams("parallel", "parallel"),
        name="proj",
    )(x, w, bias.reshape(1, ncols).astype(F32))


def _ffn_in_kernel(x_ref, wg_ref, wu_ref, o_ref):
    x = x_ref[...]
    gate = _dot(x, wg_ref[...])
    up = _dot(x, wu_ref[...])
    o_ref[...] = (gate * _sigmoid(gate) * up).astype(o_ref.dtype)


def _ffn_in(x, w_in):
    m, k = x.shape
    d_ff = w_in.shape[1] // 2
    tm = _pick(m, (512, 256, 128, 64, 32, 16, 8))
    tn = _pick(d_ff, (512, 256, 128))
    off = d_ff // tn
    return pl.pallas_call(
        _ffn_in_kernel,
        grid=(d_ff // tn, m // tm),
        in_specs=[pl.BlockSpec((tm, k), lambda j, i: (i, 0)),
                  pl.BlockSpec((k, tn), lambda j, i: (0, j)),
                  pl.BlockSpec((k, tn), lambda j, i: (0, off + j))],
        out_specs=pl.BlockSpec((tm, tn), lambda j, i: (i, j)),
        out_shape=jax.ShapeDtypeStruct((m, d_ff), BF16),
        compiler_params=_cparams("parallel", "parallel"),
        name="ffn_in",
    )(x, w_in, w_in)


def _scan_tile(step, n_lat, n_all, reverse):
    if reverse:
        return n_all - 1 - step
    n_ctx = n_all - n_lat
    return jnp.where(step < n_ctx, n_lat + step, step - n_ctx)


def _lru_conv_kernel(x_ref, p_ref, n_ref, w_ref, b_ref, o_ref, buf, *, n_lat, n_all):
    t = pl.program_id(1)
    tt = x_ref.shape[0]
    first = jnp.logical_or(t == 0, t == n_lat)
    last = jnp.logical_or(t == n_lat - 1, t == n_all - 1)
    buf[0:SUBLANES, :] = jnp.where(first, 0.0, p_ref[...])
    buf[SUBLANES:SUBLANES + tt, :] = x_ref[...]
    buf[SUBLANES + tt:2 * SUBLANES + tt, :] = jnp.where(last, 0.0, n_ref[...])
    acc = b_ref[...] + w_ref[0:1, :] * buf[SUBLANES - 2:SUBLANES - 2 + tt, :]
    acc = acc + w_ref[1:2, :] * buf[SUBLANES - 1:SUBLANES - 1 + tt, :]
    acc = acc + w_ref[2:3, :] * buf[SUBLANES:SUBLANES + tt, :]
    acc = acc + w_ref[3:4, :] * buf[SUBLANES + 1:SUBLANES + 1 + tt, :]
    o_ref[...] = acc


def _lru_conv(rec, conv_w, conv_b, n_seq):
    b, n_tot, w = rec.shape
    tt = _pick(n_tot - n_seq, (256, 128, 64, 32, 16, 8))
    assert n_seq % tt == 0 and conv_w.shape[0] == 4
    n_lat, n_all = n_seq // tt, n_tot // tt
    r8 = tt // SUBLANES
    last8 = n_tot // SUBLANES - 1
    return pl.pallas_call(
        functools.partial(_lru_conv_kernel, n_lat=n_lat, n_all=n_all),
        grid=(b, n_all),
        in_specs=[pl.BlockSpec((None, tt, w), lambda i, j: (i, j, 0)),
                  pl.BlockSpec((None, SUBLANES, w), lambda i, j: (i, jnp.maximum(j * r8 - 1, 0), 0)),
                  pl.BlockSpec((None, SUBLANES, w), lambda i, j: (i, jnp.minimum((j + 1) * r8, last8), 0)),
                  pl.BlockSpec((4, w), lambda i, j: (0, 0)),
                  pl.BlockSpec((1, w), lambda i, j: (0, 0))],
        out_specs=pl.BlockSpec((None, tt, w), lambda i, j: (i, j, 0)),
        out_shape=jax.ShapeDtypeStruct((b, n_tot, w), F32),
        scratch_shapes=[pltpu.VMEM((tt + 2 * SUBLANES, w), F32)],
        compiler_params=_cparams("parallel", "parallel"),
        name="lru_conv",
    )(rec, rec, rec, conv_w, conv_b.reshape(1, w))


def _lru_scan_kernel(*refs, reverse, combine, blk):
    if combine:
        x_ref, wc_ref, gb_ref, ll_ref, hf_ref, gate_ref, o_ref, a_s, b_s, hl_s, ac_s, carry = refs
    else:
        x_ref, wc_ref, gb_ref, ll_ref, o_ref, a_s, b_s, hl_s, ac_s, carry = refs
    tt, w = x_ref.shape
    seg = tt // SUBLANES
    nlb = w // LANES

    @pl.when(pl.program_id(1) == 0)
    def _():
        carry[...] = jnp.zeros_like(carry)

    def put(dst, n, val):
        if blk >= LANES:
            for p in range(blk // LANES):
                dst[n * (blk // LANES) + p] = val[:, p * LANES:(p + 1) * LANES]
        else:
            off = (n * blk) % LANES
            dst[(n * blk) // LANES, :, off:off + blk] = val

    sp = _softplus(-ll_ref[...])
    for n in range(w // blk):
        sl = slice(n * blk, (n + 1) * blk)
        xb = x_ref[:, sl]
        gates = _dot(xb, wc_ref[n])
        rr = _sigmoid(gates[:, :blk] + gb_ref[0:1, sl])
        ii = _sigmoid(gates[:, blk:] + gb_ref[1:2, sl])
        a = jnp.exp(-LRU_C * rr * sp[:, sl])
        put(a_s, n, a)
        put(b_s, n, jnp.sqrt(1.0 - a * a) * (ii * xb))

    def local_scan(jj, hc):
        h8, c8 = hc
        j = seg - 1 - jj if reverse else jj
        idx = pl.ds(j, SUBLANES, stride=seg)
        a8 = a_s[:, idx, :]
        h8 = a8 * h8 + b_s[:, idx, :]
        c8 = a8 * c8
        hl_s[:, idx, :] = h8
        ac_s[:, idx, :] = c8
        return h8, c8

    h8, c8 = lax.fori_loop(0, seg, local_scan,
                           (jnp.zeros((nlb, SUBLANES, LANES), F32),
                            jnp.ones((nlb, SUBLANES, LANES), F32)))
    c = carry[...]
    starts = [None] * SUBLANES
    for s in (range(SUBLANES - 1, -1, -1) if reverse else range(SUBLANES)):
        starts[s] = c
        c = h8[:, s:s + 1, :] + c8[:, s:s + 1, :] * c
    carry[...] = c
    start8 = jnp.concatenate(starts, axis=1)

    def fix(j, _):
        idx = pl.ds(j, SUBLANES, stride=seg)
        hl_s[:, idx, :] = hl_s[:, idx, :] + ac_s[:, idx, :] * start8
        return 0

    lax.fori_loop(0, seg, fix, 0)
    for n in range(nlb):
        sl = slice(n * LANES, (n + 1) * LANES)
        if combine:
            o_ref[:, sl] = ((hl_s[n] + hf_ref[:, sl]) * gate_ref[:, sl]).astype(o_ref.dtype)
        else:
            o_ref[:, sl] = hl_s[n]


def _lru_scan(x, wcat, gate_b, log_lambda, n_seq, reverse, h_fwd=None, gate=None):
    b, n_tot, w = x.shape
    blk = w // LRU_BLOCKS
    tt = _pick(n_tot - n_seq, (256, 128, 64))
    assert n_seq % tt == 0 and tt % (SUBLANES * SUBLANES) == 0
    n_lat, n_all = n_seq // tt, n_tot // tt
    combine = h_fwd is not None
    tile = lambda i, j: (i, _scan_tile(j, n_lat, n_all, reverse), 0)
    in_specs = [pl.BlockSpec((None, tt, w), tile),
                pl.BlockSpec((LRU_BLOCKS, blk, 2 * blk), lambda i, j: (0, 0, 0)),
                pl.BlockSpec((2, w), lambda i, j: (0, 0)),
                pl.BlockSpec((1, w), lambda i, j: (0, 0))]
    args = [x, wcat, gate_b, log_lambda.reshape(1, w)]
    if combine:
        in_specs += [pl.BlockSpec((None, tt, w), tile), pl.BlockSpec((None, tt, w), tile)]
        args += [h_fwd, gate]
    return pl.pallas_call(
        functools.partial(_lru_scan_kernel, reverse=reverse, combine=combine, blk=blk),
        grid=(b, n_all),
        in_specs=in_specs,
        out_specs=pl.BlockSpec((None, tt, w), tile),
        out_shape=jax.ShapeDtypeStruct((b, n_tot, w), BF16 if combine else F32),
        scratch_shapes=([pltpu.VMEM((w // LANES, tt, LANES), F32)] * 4
                        + [pltpu.VMEM((w // LANES, 1, LANES), F32)]),
        compiler_params=_cparams("parallel", "arbitrary"),
        name="lru_scan_bwd" if reverse else "lru_scan_fwd",
    )(*args)


def _rglru_mixer(us, n_seq, w_in, b_in, conv_w, conv_b, gate_w, gate_b, log_lambda, w_out, b_out):
    b, n_tot, d = us.shape
    w = w_in.shape[1] // 2
    blk = w // LRU_BLOCKS
    x2 = us.reshape(b * n_tot, d)
    w_in = w_in.astype(BF16)
    gate = _mm(x2, w_in, b_in[:w], act="gelu", col0=0, ncols=w).reshape(b, n_tot, w)
    rec = _mm(x2, w_in, b_in[w:], col0=w, ncols=w).reshape(b, n_tot, w)
    x = _lru_conv(rec, conv_w, conv_b, n_seq)
    wcat = jnp.concatenate([gate_w[:, 0], gate_w[:, 1]], axis=-1).astype(BF16)
    h_f = _lru_scan(x, wcat[0], gate_b[0], log_lambda[0], n_seq, False)
    hy = _lru_scan(x, wcat[1], gate_b[1], log_lambda[1], n_seq, True, h_fwd=h_f, gate=gate)
    y = _mm(hy.reshape(b * n_tot, w), w_out.astype(BF16), b_out)
    return y.reshape(b, n_tot, d)


def _gla_kernel(*refs, reverse, combine, scale):
    if combine:
        q_ref, k_ref, v_ref, g_ref, of_ref, o_ref, st = refs
    else:
        q_ref, k_ref, v_ref, g_ref, o_ref, st = refs
    tt = q_ref.shape[0]
    c = GLA_CHUNK

    @pl.when(pl.program_id(2) == 0)
    def _():
        st[...] = jnp.zeros_like(st)

    row = lax.broadcasted_iota(jnp.int32, (c, c), 0)
    col = lax.broadcasted_iota(jnp.int32, (c, c), 1)
    tri = (row <= col) if reverse else (row >= col)
    tri_f = tri.astype(F32)
    n_chunks = tt // c
    for ci in (range(n_chunks - 1, -1, -1) if reverse else range(n_chunks)):
        sl = slice(ci * c, (ci + 1) * c)
        g = g_ref[sl, :]
        k = k_ref[sl, :]
        v = v_ref[sl, :]
        bcum = _dot_f32(tri_f, g)
        b_last = bcum[0:1, :] if reverse else bcum[c - 1:c, :]
        q_e = q_ref[sl, :] * scale * jnp.exp(bcum)
        k_e = k * jnp.exp(-bcum)
        att = jnp.where(tri, _dot_nt(q_e, k_e), 0.0)
        s_t = st[...]
        o = _dot(att, v) + _dot_nt(q_e, s_t)
        if combine:
            o = o + of_ref[sl, :]
        o_ref[sl, :] = o
        k_rem = k * jnp.exp(b_last - bcum)
        st[...] = s_t * jnp.exp(b_last) + _dot_tn(v, k_rem)


def _gla_scan(qkvr, g, n_seq, reverse, o_fwd=None):
    b, n_tot, _ = qkvr.shape
    dk = g.shape[-1]
    hk = dk // GLA_HEADS
    hv = 2 * hk
    tt = _pick(n_tot - n_seq, (256, 128, 64))
    assert n_seq % tt == 0
    n_lat, n_all = n_seq // tt, n_tot // tt
    combine = o_fwd is not None
    kq, kv = dk // hk, 2 * dk // hv
    tile = lambda i, h, j: _scan_tile(j, n_lat, n_all, reverse)
    in_specs = [pl.BlockSpec((None, tt, hk), lambda i, h, j: (i, tile(i, h, j), h)),
                pl.BlockSpec((None, tt, hk), lambda i, h, j: (i, tile(i, h, j), kq + h)),
                pl.BlockSpec((None, tt, hv), lambda i, h, j: (i, tile(i, h, j), kv + h)),
                pl.BlockSpec((None, tt, hk), lambda i, h, j: (i, tile(i, h, j), h))]
    args = [qkvr, qkvr, qkvr, g]
    o_spec = pl.BlockSpec((None, tt, hv), lambda i, h, j: (i, tile(i, h, j), h))
    if combine:
        in_specs.append(o_spec)
        args.append(o_fwd)
    return pl.pallas_call(
        functools.partial(_gla_kernel, reverse=reverse, combine=combine, scale=float(hk) ** -0.5),
        grid=(b, GLA_HEADS, n_all),
        in_specs=in_specs,
        out_specs=o_spec,
        out_shape=jax.ShapeDtypeStruct((b, n_tot, 2 * dk), F32),
        scratch_shapes=[pltpu.VMEM((hv, hk), F32)],
        compiler_params=_cparams("parallel", "parallel", "arbitrary"),
        name="gla_bwd" if reverse else "gla_fwd",
    )(*args)


def _gla_read_kernel(o_ref, r_ref, br_ref, g_ref, y_ref, *, hv):
    for h in range(GLA_HEADS):
        sl = slice(h * hv, (h + 1) * hv)
        o = o_ref[:, sl]
        o = o * lax.rsqrt(jnp.mean(o * o, axis=-1, keepdims=True) + NORM_EPS) * g_ref[:, sl]
        r = r_ref[:, sl] + br_ref[:, sl]
        y_ref[:, sl] = (o * (r * _sigmoid(r))).astype(y_ref.dtype)


def _gla_read(o, qkvr, b_r, norm_g):
    m, dv = o.shape
    tm = _pick(m, (256, 128, 64, 32, 16, 8))
    r_off = qkvr.shape[1] // dv - 1
    return pl.pallas_call(
        functools.partial(_gla_read_kernel, hv=dv // GLA_HEADS),
        grid=(m // tm,),
        in_specs=[pl.BlockSpec((tm, dv), lambda i: (i, 0)),
                  pl.BlockSpec((tm, dv), lambda i: (i, r_off)),
                  pl.BlockSpec((1, dv), lambda i: (0, 0)),
                  pl.BlockSpec((1, dv), lambda i: (0, 0))],
        out_specs=pl.BlockSpec((tm, dv), lambda i: (i, 0)),
        out_shape=jax.ShapeDtypeStruct((m, dv), BF16),
        compiler_params=_cparams("parallel"),
        name="gla_readout",
    )(o, qkvr, b_r.reshape(1, dv), norm_g.reshape(1, dv))


def _gla_mixer(us, n_seq, w_in, b_r, gate_w1, gate_w2, gate_b, norm_g, w_out):
    b, n_tot, d = us.shape
    m = b * n_tot
    x2 = us.reshape(m, d)
    dk = gate_w2.shape[-1]
    rank = gate_w1.shape[-1]
    qkvr = _mm(x2, w_in.astype(BF16))
    low = _mm(x2, jnp.concatenate([gate_w1[0], gate_w1[1]], axis=-1).astype(BF16))
    zeros = jnp.zeros_like(gate_w2[0])
    o = None
    for dr in range(2):
        w2 = jnp.concatenate([gate_w2[0], zeros] if dr == 0 else [zeros, gate_w2[1]], axis=0)
        g = _mm(low, w2.astype(BF16), gate_b[dr], act="gla_gate")
        o = _gla_scan(qkvr.reshape(b, n_tot, -1), g.reshape(b, n_tot, dk), n_seq, dr == 1, o_fwd=o)
    del rank
    hy = _gla_read(o.reshape(m, 2 * dk), qkvr, b_r, norm_g)
    return _mm(hy, w_out.astype(BF16)).reshape(b, n_tot, d)


def _rwkv_mix_kernel(x_ref, p_ref, n_ref, mu_ref, *o_refs, n_lat, n_all):
    t = pl.program_id(1)
    tt = x_ref.shape[0]
    first = jnp.logical_or(t == 0, t == n_lat)
    last = jnp.logical_or(t == n_lat - 1, t == n_all - 1)
    x = x_ref[...]
    prev_row = jnp.where(first, 0.0, p_ref[SUBLANES - 1:SUBLANES, :])
    next_row = jnp.where(last, 0.0, n_ref[0:1, :])
    rid = lax.broadcasted_iota(jnp.int32, x.shape, 0)
    up = jnp.where(rid == 0, prev_row, pltpu.roll(x, 1, 0))
    dn = jnp.where(rid == tt - 1, next_row, pltpu.roll(x, tt - 1, 0))
    dx = 0.5 * (up + dn) - x
    for i, o_ref in enumerate(o_refs):
        o_ref[...] = (x + dx * mu_ref[i:i + 1, :]).astype(o_ref.dtype)


def _rwkv_mix(us, mu, n_seq):
    b, n_tot, d = us.shape
    n_mix = mu.shape[0]
    tt = _pick(n_tot - n_seq, (256, 128, 64, 32, 16, 8))
    assert n_seq % tt == 0
    n_lat, n_all = n_seq // tt, n_tot // tt
    r8 = tt // SUBLANES
    last8 = n_tot // SUBLANES - 1
    out = jax.ShapeDtypeStruct((b, n_tot, d), BF16)
    return pl.pallas_call(
        functools.partial(_rwkv_mix_kernel, n_lat=n_lat, n_all=n_all),
        grid=(b, n_all),
        in_specs=[pl.BlockSpec((None, tt, d), lambda i, j: (i, j, 0)),
                  pl.BlockSpec((None, SUBLANES, d), lambda i, j: (i, jnp.maximum(j * r8 - 1, 0), 0)),
                  pl.BlockSpec((None, SUBLANES, d), lambda i, j: (i, jnp.minimum((j + 1) * r8, last8), 0)),
                  pl.BlockSpec((n_mix, d), lambda i, j: (0, 0))],
        out_specs=[pl.BlockSpec((None, tt, d), lambda i, j: (i, j, 0))] * n_mix,
        out_shape=[out] * n_mix,
        compiler_params=_cparams("parallel", "parallel"),
        name="rwkv_mix",
    )(us, us, us, mu)


def _stack_heads(x, head0):
    return jnp.concatenate([jnp.where(head0, x, 0.0), jnp.where(head0, 0.0, x)], axis=0)


def _rwkv_scan_kernel(r_ref, k_ref, v_ref, a_ref, w_ref, kk_ref, ka_ref, o_ref, z_ref, *, reverse):
    tt = r_ref.shape[0]
    c = RWKV_CHUNK
    lanes = 2 * RWKV_N

    @pl.when(pl.program_id(2) == 0)
    def _():
        z_ref[...] = jnp.zeros_like(z_ref)

    lane = lax.broadcasted_iota(jnp.int32, (1, lanes), 1)
    head0 = lane < RWKV_N
    r2 = lax.broadcasted_iota(jnp.int32, (lanes, lanes), 0)
    c2 = lax.broadcasted_iota(jnp.int32, (lanes, lanes), 1)
    same_head = (r2 < RWKV_N) == (c2 < RWKV_N)
    same_head_f = same_head.astype(F32)
    eye2 = (r2 == c2).astype(F32)
    tr = lax.broadcasted_iota(jnp.int32, (c, c), 0)
    tc = lax.broadcasted_iota(jnp.int32, (c, c), 1)
    tri_f = ((tr <= tc) if reverse else (tr >= tc)).astype(F32)
    pr = lax.broadcasted_iota(jnp.int32, (2 * c, 4 * c), 0)
    pc = lax.broadcasted_iota(jnp.int32, (2 * c, 4 * c), 1) % c
    incl = (pr >= c).astype(jnp.int32)
    pt = pr % c
    pmask = (pt - incl < pc) if reverse else (pt + incl > pc)
    eye_side = (lax.broadcasted_iota(jnp.int32, (c, 2 * c), 0)
                == lax.broadcasted_iota(jnp.int32, (c, 2 * c), 1) % c).astype(F32)
    kk_scale = kk_ref[...]
    ka_scale = ka_ref[...]

    n_chunks = tt // c
    for ci in (range(n_chunks - 1, -1, -1) if reverse else range(n_chunks)):
        sl = slice(ci * c, (ci + 1) * c)
        r = r_ref[sl, :]
        k = k_ref[sl, :]
        v = v_ref[sl, :]
        asig = a_ref[sl, :]
        ld = w_ref[sl, :]
        kk = k * kk_scale
        kk = kk * lax.rsqrt(jnp.maximum(_dot_f32(kk * kk, same_head_f), 1e-24))
        a = -kk
        bvec = kk * asig
        kd = k * (1.0 + (asig - 1.0) * ka_scale)
        cum = _dot_f32(tri_f, ld)
        tot = cum[0:1, :] if reverse else cum[c - 1:c, :]
        e_neg = jnp.exp(-cum)
        e_rem = jnp.exp(tot - cum)
        at = a * jnp.exp(cum - ld)
        rt = r * jnp.exp(cum)
        bt = bvec * e_neg
        kt = kd * e_neg
        bc = bvec * e_rem
        kc = kd * e_rem
        lhs = jnp.concatenate([at, rt], axis=0)
        rhs = jnp.concatenate([_stack_heads(bt, head0), _stack_heads(kt, head0)], axis=0)
        p = jnp.where(pmask, _dot_nt(lhs, rhs), 0.0)
        a_ab = p[:c, :2 * c]
        a_ak = p[:c, 2 * c:]
        p_rb = p[c:, :2 * c]
        p_rk = p[c:, 2 * c:]
        inv = eye_side + a_ab
        pw = a_ab
        for _ in range(5):
            pw = _dot_f32(pw, _stack_heads(pw, head0))
            inv = inv + _dot_f32(inv, _stack_heads(pw, head0))
        at2 = _dot_f32(inv, _stack_heads(at, head0))
        av = _dot(a_ak, _stack_heads(v, head0))
        u_loc = _dot_f32(inv, _stack_heads(av, head0))
        r_eff = rt + _dot(p_rb, _stack_heads(at2, head0))
        y_loc = _dot(p_rb, _stack_heads(u_loc, head0)) + _dot(p_rk, _stack_heads(v, head0))
        m_t = eye2 * jnp.exp(tot) + jnp.where(same_head, _dot_tn(bc, at2), 0.0)
        n_t = jnp.where(same_head, _dot_tn(bc, u_loc) + _dot_tn(kc, v), 0.0)
        z = z_ref[...]
        o_ref[sl, :] = _dot_f32(r_eff, z) + y_loc
        z_ref[...] = _dot_f32(m_t, z) + n_t


def _rwkv_scan(r, k, v, asig, logw, k_k, k_a, n_seq, reverse):
    b, n_tot, d = r.shape
    lanes = 2 * RWKV_N
    tt = _pick(n_tot - n_seq, (256, 128, 64))
    assert n_seq % tt == 0 and d % lanes == 0
    n_lat, n_all = n_seq // tt, n_tot // tt
    tile = pl.BlockSpec((None, tt, lanes),
                        lambda i, h, j: (i, _scan_tile(j, n_lat, n_all, reverse), h))
    vec = pl.BlockSpec((1, lanes), lambda i, h, j: (0, h))
    return pl.pallas_call(
        functools.partial(_rwkv_scan_kernel, reverse=reverse),
        grid=(b, d // lanes, n_all),
        in_specs=[tile] * 5 + [vec, vec],
        out_specs=tile,
        out_shape=jax.ShapeDtypeStruct((b, n_tot, d), F32),
        scratch_shapes=[pltpu.VMEM((lanes, lanes), F32)],
        compiler_params=_cparams("parallel", "parallel", "arbitrary"),
        name="rwkv_bwd" if reverse else "rwkv_fwd",
    )(r, k, v, asig, logw, k_k.reshape(1, d), k_a.reshape(1, d))


def _rwkv_read_kernel(y0_ref, y1_ref, r_ref, k_ref, v_ref, a0_ref, a1_ref, g_ref,
                      ka_ref, rk_ref, lw_ref, lb_ref, o_ref):
    lanes = 2 * RWKV_N
    r2 = lax.broadcasted_iota(jnp.int32, (lanes, lanes), 0)
    c2 = lax.broadcasted_iota(jnp.int32, (lanes, lanes), 1)
    same_head_f = ((r2 < RWKV_N) == (c2 < RWKV_N)).astype(F32)
    y = y0_ref[...] + y1_ref[...]
    mean = _dot_f32(y, same_head_f) * (1.0 / RWKV_N)
    yc = y - mean
    var = _dot_f32(yc * yc, same_head_f) * (1.0 / RWKV_N)
    y = yc * lax.rsqrt(var + RWKV_GN_EPS) * lw_ref[...] + lb_ref[...]
    r = r_ref[...]
    k = k_ref[...]
    ka = ka_ref[...]
    k0 = k * (1.0 + (a0_ref[...] - 1.0) * ka)
    k1 = k * (1.0 + (a1_ref[...] - 1.0) * ka)
    rk = rk_ref[...]
    bonus = (_dot_f32(r * k0 * rk, same_head_f) + _dot_f32(r * k1 * rk, same_head_f)) * v_ref[...]
    o_ref[...] = ((y + bonus) * g_ref[...]).astype(o_ref.dtype)


def _rwkv_read(y0, y1, r, k, v, a0, a1, g, k_a, r_k, ln_w, ln_b):
    m, d = r.shape
    lanes = 2 * RWKV_N
    tm = _pick(m, (512, 256, 128, 64, 32, 16, 8))
    tile = pl.BlockSpec((tm, lanes), lambda h, i: (i, h))
    vec = pl.BlockSpec((1, lanes), lambda h, i: (0, h))
    return pl.pallas_call(
        _rwkv_read_kernel,
        grid=(d // lanes, m // tm),
        in_specs=[tile] * 8 + [vec] * 4,
        out_specs=tile,
        out_shape=jax.ShapeDtypeStruct((m, d), BF16),
        compiler_params=_cparams("parallel", "parallel"),
        name="rwkv_readout",
    )(y0, y1, r, k, v, a0, a1, g, k_a.reshape(1, d), r_k.reshape(1, d),
      ln_w.reshape(1, d), ln_b.reshape(1, d))


def _pad_rows(w2, dr):
    zeros = jnp.zeros_like(w2[0])
    return jnp.concatenate([w2[0], zeros] if dr == 0 else [zeros, w2[1]], axis=0).astype(BF16)


def _rwkv_mixer(us, n_seq, mu, w_rkv, w0, w1, w2, a0, a1, a2, g1, g2, k_k, k_a, r_k,
                ln_w, ln_b, w_out):
    b, n_tot, d = us.shape
    m = b * n_tot
    xr, xw, xk, xv, xa, xg = [t.reshape(m, d) for t in _rwkv_mix(us, mu, n_seq)]
    w_rkv = w_rkv.astype(BF16)
    r = _mm(xr, w_rkv[0])
    k = _mm(xk, w_rkv[1])
    v = _mm(xv, w_rkv[2])
    w_low = _mm(xw, jnp.concatenate([w1[0], w1[1]], axis=-1).astype(BF16), act="tanh")
    a_low = _mm(xa, jnp.concatenate([a1[0], a1[1]], axis=-1).astype(BF16))
    g_low = _mm(xg, g1.astype(BF16), act="sigmoid")
    gate = _mm(g_low, g2.astype(BF16))
    sh = (b, n_tot, d)
    ys, asigs = [], []
    for dr in range(2):
        logw = _mm(w_low, _pad_rows(w2, dr), w0[dr], act="rwkv_decay")
        asig = _mm(a_low, _pad_rows(a2, dr), a0[dr], act="sigmoid")
        asigs.append(asig)
        ys.append(_rwkv_scan(r.reshape(sh), k.reshape(sh), v.reshape(sh), asig.reshape(sh),
                             logw.reshape(sh), k_k, k_a, n_seq, dr == 1).reshape(m, d))
    hy = _rwkv_read(ys[0], ys[1], r, k, v, asigs[0], asigs[1], gate, k_a, r_k.reshape(-1),
                    ln_w, ln_b)
    return _mm(hy, w_out.astype(BF16)).reshape(b, n_tot, d)


def kernel(x, c, ctx, c_ctx, ada_w, ada_b, norm_g, ffn_w_in, ffn_w_out, lru_w_in, lru_b_in, lru_conv_w, lru_conv_b, lru_gate_w, lru_gate_b, lru_log_lambda, lru_w_out, lru_b_out, gla_w_in, gla_b_r, gla_gate_w1, gla_gate_w2, gla_gate_b, gla_norm_g, gla_w_out, rwkv_mu, rwkv_w_rkv, rwkv_w0, rwkv_w1, rwkv_w2, rwkv_a0, rwkv_a1, rwkv_a2, rwkv_g1, rwkv_g2, rwkv_k_k, rwkv_k_a, rwkv_r_k, rwkv_ln_w, rwkv_ln_b, rwkv_w_out):
    b, n_seq, d = x.shape
    depth = ada_w.shape[0]
    assert b < 8
    cond = jnp.concatenate([c, c_ctx[None, :], jnp.zeros((8 - b - 1, d), F32)], axis=0)
    mods = _mods(cond, ada_w, ada_b).reshape(depth, 8, N_MOD, d)
    h_lat, h_ctx = x, ctx
    for i in range(depth):
        ctx_out = i < depth - 1
        kind, j = i % N_MIXERS, i // N_MIXERS
        col_major = i % 2 == 1
        m_lat, m_ctx = mods[i, :b], mods[i, b]
        us = _norm_mod(h_lat, h_ctx, norm_g[i, 0], m_lat, m_ctx, 0, 1, col_major,
                       F32 if kind == 2 else BF16)
        if kind == 0:
            y = _rglru_mixer(us, n_seq, lru_w_in[j], lru_b_in[j], lru_conv_w[j], lru_conv_b[j],
                             lru_gate_w[j], lru_gate_b[j], lru_log_lambda[j], lru_w_out[j],
                             lru_b_out[j])
        elif kind == 1:
            y = _gla_mixer(us, n_seq, gla_w_in[j], gla_b_r[j], gla_gate_w1[j], gla_gate_w2[j],
                           gla_gate_b[j], gla_norm_g[j], gla_w_out[j])
        else:
            y = _rwkv_mixer(us, n_seq, rwkv_mu[j], rwkv_w_rkv[j], rwkv_w0[j], rwkv_w1[j],
                            rwkv_w2[j], rwkv_a0[j], rwkv_a1[j], rwkv_a2[j], rwkv_g1[j],
                            rwkv_g2[j], rwkv_k_k[j], rwkv_k_a[j], rwkv_r_k[j], rwkv_ln_w[j],
                            rwkv_ln_b[j], rwkv_w_out[j])
        h_lat, h_ctx = _resid(h_lat, h_ctx if ctx_out else None, y, norm_g[i, 1], m_lat, m_ctx,
                              2, col_major)
        us = _norm_mod(h_lat, h_ctx, norm_g[i, 2], m_lat, m_ctx, 3, 4, False, BF16)
        n_tot = us.shape[1]
        mid = _ffn_in(us.reshape(b * n_tot, d), ffn_w_in[i].astype(BF16))
        y = _mm(mid, ffn_w_out[i].astype(BF16)).reshape(b, n_tot, d)
        h_lat, h_ctx = _resid(h_lat, h_ctx, y, norm_g[i, 3], m_lat, m_ctx, 5, False)
    return h_lat
```

```python
import functools

import jax
import jax.numpy as jnp
from jax import lax
from jax.experimental import pallas as pl
from jax.experimental.pallas import tpu as pltpu

F32 = jnp.float32
BF16 = jnp.bfloat16

GRID_W = 64
N_MIXERS = 3
N_MOD = 6
NORM_EPS = 1e-6
LRU_BLOCKS = 16
LRU_C = 8.0
GLA_HEADS = 4
GLA_TAU = 16.0
GLA_CHUNK = 64
RWKV_N = 64
RWKV_CHUNK = 64
RWKV_GN_EPS = 64e-5
LANES = 128
SUBLANES = 8
VMEM_LIMIT = 52 * 1024 * 1024
HIGHEST = lax.Precision.HIGHEST


def _cparams(*sem):
    return pltpu.CompilerParams(dimension_semantics=sem, vmem_limit_bytes=VMEM_LIMIT)


def _pick(n, cands):
    for c in cands:
        if n % c == 0:
            return c
    return n


def _dot(a, b):
    return jnp.dot(a.astype(BF16), b.astype(BF16), preferred_element_type=F32)


def _dot_nt(a, b):
    return lax.dot_general(a.astype(BF16), b.astype(BF16), (((1,), (1,)), ((), ())),
                           preferred_element_type=F32)


def _dot_tn(a, b):
    return lax.dot_general(a.astype(BF16), b.astype(BF16), (((0,), (0,)), ((), ())),
                           preferred_element_type=F32)


def _dot_f32(a, b):
    return jnp.dot(a, b, precision=HIGHEST, preferred_element_type=F32)


def _sigmoid(x):
    return 1.0 / (1.0 + jnp.exp(-x))


def _softplus(x):
    return jnp.maximum(x, 0.0) + jnp.log(1.0 + jnp.exp(-jnp.abs(x)))


def _mods_kernel(x_ref, w_ref, b_ref, o_ref):
    x = x_ref[...]
    x = x * _sigmoid(x)
    o_ref[...] = _dot(x, w_ref[...]) + b_ref[...]


def _mods(cond, ada_w, ada_b):
    n_layers, d, n = ada_w.shape
    tn = _pick(n, (1024, 512, 256, 128))
    return pl.pallas_call(
        _mods_kernel,
        grid=(n_layers, n // tn),
        in_specs=[pl.BlockSpec((8, d), lambda l, j: (0, 0)),
                  pl.BlockSpec((None, d, tn), lambda l, j: (l, 0, j)),
                  pl.BlockSpec((None, 1, tn), lambda l, j: (l, 0, j))],
        out_specs=pl.BlockSpec((None, 8, tn), lambda l, j: (l, 0, j)),
        out_shape=jax.ShapeDtypeStruct((n_layers, 8, n), F32),
        compiler_params=_cparams("parallel", "parallel"),
        name="ada_mods",
    )(cond, ada_w, ada_b.reshape(n_layers, 1, n))


def _norm_mod_kernel(*refs, k_shift, k_scale, aliased):
    if aliased:
        h_ref, g_ref, m_ref, _, o_ref = refs
    else:
        h_ref, g_ref, m_ref, o_ref = refs
    x = h_ref[...]
    y = x * lax.rsqrt(jnp.mean(x * x, axis=-1, keepdims=True) + NORM_EPS) * g_ref[...]
    y = y * (1.0 + m_ref[k_scale:k_scale + 1, :]) + m_ref[k_shift:k_shift + 1, :]
    o_ref[...] = y.astype(o_ref.dtype)


def _norm_mod(h_lat, h_ctx, g, m_lat, m_ctx, k_shift, k_scale, col_major, out_dtype):
    b, n, d = h_lat.shape
    rows = n // GRID_W
    n_ctx = 0 if h_ctx is None else h_ctx.shape[1]
    g = g.reshape(1, d)
    out_shape = jax.ShapeDtypeStruct((b, n + n_ctx, d), out_dtype)
    kern = functools.partial(_norm_mod_kernel, k_shift=k_shift, k_scale=k_scale)
    if col_major:
        tt = rows
        grid = (b, GRID_W)
        h_in = h_lat.reshape(b, rows, GRID_W * d)
        h_spec = pl.BlockSpec((None, rows, d), lambda i, j: (i, 0, j))
    else:
        tt = _pick(n, (256, 128, 64, 32, 16, 8))
        grid = (b, n // tt)
        h_in = h_lat
        h_spec = pl.BlockSpec((None, tt, d), lambda i, j: (i, j, 0))
    us = pl.pallas_call(
        functools.partial(kern, aliased=False),
        grid=grid,
        in_specs=[h_spec,
                  pl.BlockSpec((1, d), lambda i, j: (0, 0)),
                  pl.BlockSpec((None, N_MOD, d), lambda i, j: (i, 0, 0))],
        out_specs=pl.BlockSpec((None, tt, d), lambda i, j: (i, j, 0)),
        out_shape=out_shape,
        compiler_params=_cparams("parallel", "parallel"),
        name="norm_mod_lat",
    )(h_in, g, m_lat)
    if h_ctx is None:
        return us
    tc = _pick(n_ctx, (256, 128, 64, 32, 16, 8))
    assert n % tc == 0
    off = n // tc
    return pl.pallas_call(
        functools.partial(kern, aliased=True),
        grid=(b, n_ctx // tc),
        in_specs=[pl.BlockSpec((None, tc, d), lambda i, j: (i, j, 0)),
                  pl.BlockSpec((1, d), lambda i, j: (0, 0)),
                  pl.BlockSpec((None, N_MOD, d), lambda i, j: (0, 0, 0)),
                  pl.BlockSpec(memory_space=pl.ANY)],
        out_specs=pl.BlockSpec((None, tc, d), lambda i, j: (i, off + j, 0)),
        out_shape=out_shape,
        input_output_aliases={3: 0},
        compiler_params=_cparams("parallel", "parallel"),
        name="norm_mod_ctx",
    )(h_ctx, g, m_ctx.reshape(1, N_MOD, d), us)


def _resid_kernel(h_ref, y_ref, g_ref, m_ref, o_ref, *, k_gate):
    y = y_ref[...]
    yn = y * lax.rsqrt(jnp.mean(y * y, axis=-1, keepdims=True) + NORM_EPS) * g_ref[...]
    o_ref[...] = h_ref[...] + m_ref[k_gate:k_gate + 1, :] * yn


def _resid(h_lat, h_ctx, y, g, m_lat, m_ctx, k_gate, col_major):
    b, n, d = h_lat.shape
    rows = n // GRID_W
    g = g.reshape(1, d)
    kern = functools.partial(_resid_kernel, k_gate=k_gate)
    if col_major:
        tt = rows
        grid = (b, GRID_W)
        h_in = h_lat.reshape(b, rows, GRID_W * d)
        h_spec = pl.BlockSpec((None, rows, d), lambda i, j: (i, 0, j))
        out_shape = jax.ShapeDtypeStruct((b, rows, GRID_W * d), F32)
    else:
        tt = _pick(n, (256, 128, 64, 32, 16, 8))
        grid = (b, n // tt)
        h_in = h_lat
        h_spec = pl.BlockSpec((None, tt, d), lambda i, j: (i, j, 0))
        out_shape = jax.ShapeDtypeStruct((b, n, d), F32)
    new_lat = pl.pallas_call(
        kern,
        grid=grid,
        in_specs=[h_spec,
                  pl.BlockSpec((None, tt, d), lambda i, j: (i, j, 0)),
                  pl.BlockSpec((1, d), lambda i, j: (0, 0)),
                  pl.BlockSpec((None, N_MOD, d), lambda i, j: (i, 0, 0))],
        out_specs=h_spec,
        out_shape=out_shape,
        compiler_params=_cparams("parallel", "parallel"),
        name="resid_lat",
    )(h_in, y, g, m_lat).reshape(b, n, d)
    if h_ctx is None:
        return new_lat, None
    n_ctx = h_ctx.shape[1]
    tc = _pick(n_ctx, (256, 128, 64, 32, 16, 8))
    off = n // tc
    new_ctx = pl.pallas_call(
        kern,
        grid=(b, n_ctx // tc),
        in_specs=[pl.BlockSpec((None, tc, d), lambda i, j: (i, j, 0)),
                  pl.BlockSpec((None, tc, d), lambda i, j: (i, off + j, 0)),
                  pl.BlockSpec((1, d), lambda i, j: (0, 0)),
                  pl.BlockSpec((None, N_MOD, d), lambda i, j: (0, 0, 0))],
        out_specs=pl.BlockSpec((None, tc, d), lambda i, j: (i, j, 0)),
        out_shape=jax.ShapeDtypeStruct((b, n_ctx, d), F32),
        compiler_params=_cparams("parallel", "parallel"),
        name="resid_ctx",
    )(h_ctx, y, g, m_ctx.reshape(1, N_MOD, d))
    return new_lat, new_ctx


def _gelu_tanh(x):
    return 0.5 * x * (1.0 + jnp.tanh(0.7978845608028654 * (x + 0.044715 * x * x * x)))


def _apply_act(z, act):
    if act is None:
        return z
    if act == "gelu":
        return _gelu_tanh(z)
    if act == "tanh":
        return jnp.tanh(z)
    if act == "sigmoid":
        return _sigmoid(z)
    if act == "gla_gate":
        return -_softplus(-z) * (1.0 / GLA_TAU)
    if act == "rwkv_decay":
        return -jnp.exp(-_softplus(-z) - 0.5)
    raise ValueError(act)


def _mm_kernel(x_ref, w_ref, b_ref, o_ref, *, act):
    z = _dot(x_ref[...], w_ref[...]) + b_ref[...]
    o_ref[...] = _apply_act(z, act).astype(o_ref.dtype)


def _mm(x, w, bias=None, act=None, out_dtype=F32, col0=0, ncols=None):
    m, k = x.shape
    n_total = w.shape[1]
    ncols = n_total if ncols is None else ncols
    tm = _pick(m, (512, 256, 128, 64, 32, 16, 8))
    tn = _pick(ncols, (1024, 512, 256, 128)) if k <= 2048 else _pick(ncols, (512, 256, 128))
    assert col0 % tn == 0
    off = col0 // tn
    if bias is None:
        bias = jnp.zeros((ncols,), F32)
    return pl.pallas_call(
        functools.partial(_mm_kernel, act=act),
        grid=(ncols // tn, m // tm),
        in_specs=[pl.BlockSpec((tm, k), lambda j, i: (i, 0)),
                  pl.BlockSpec((k, tn), lambda j, i: (0, off + j)),
                  pl.BlockSpec((1, tn), lambda j, i: (0, j))],
        out_specs=pl.BlockSpec((tm, tn), lambda j, i: (i, j)),
        out_shape=jax.ShapeDtypeStruct((m, ncols), out_dtype),
        compiler_params=_cparams("parallel", "parallel"),
        name="proj",
    )(x, w, bias.reshape(1, ncols).astype(F32))


def _ffn_in_kernel(x_ref, wg_ref, wu_ref, o_ref):
    x = x_ref[...]
    gate = _dot(x, wg_ref[...])
    up = _dot(x, wu_ref[...])
    o_ref[...] = (gate * _sigmoid(gate) * up).astype(o_ref.dtype)


def _ffn_in(x, w_in):
    m, k = x.shape
    d_ff = w_in.shape[1] // 2
    tm = _pick(m, (512, 256, 128, 64, 32, 16, 8))
    tn = _pick(d_ff, (512, 256, 128))
    off = d_ff // tn
    return pl.pallas_call(
        _ffn_in_kernel,
        grid=(d_ff // tn, m // tm),
        in_specs=[pl.BlockSpec((tm, k), lambda j, i: (i, 0)),
                  pl.BlockSpec((k, tn), lambda j, i: (0, j)),
                  pl.BlockSpec((k, tn), lambda j, i: (0, off + j))],
        out_specs=pl.BlockSpec((tm, tn), lambda j, i: (i, j)),
        out_shape=jax.ShapeDtypeStruct((m, d_ff), BF16),
        compiler_params=_cparams("parallel", "parallel"),
        name="ffn_in",
    )(x, w_in, w_in)


def _scan_tile(step, n_lat, n_all, reverse):
    if reverse:
        return n_all - 1 - step
    n_ctx = n_all - n_lat
    return jnp.where(step < n_ctx, n_lat + step, step - n_ctx)


def _lru_conv_kernel(x_ref, p_ref, n_ref, w_ref, b_ref, o_ref, buf, *, n_lat, n_all):
    t = pl.program_id(1)
    tt = x_ref.shape[0]
    first = jnp.logical_or(t == 0, t == n_lat)
    last = jnp.logical_or(t == n_lat - 1, t == n_all - 1)
    buf[0:SUBLANES, :] = jnp.where(first, 0.0, p_ref[...])
    buf[SUBLANES:SUBLANES + tt, :] = x_ref[...]
    buf[SUBLANES + tt:2 * SUBLANES + tt, :] = jnp.where(last, 0.0, n_ref[...])
    acc = b_ref[...] + w_ref[0:1, :] * buf[SUBLANES - 2:SUBLANES - 2 + tt, :]
    acc = acc + w_ref[1:2, :] * buf[SUBLANES - 1:SUBLANES - 1 + tt, :]
    acc = acc + w_ref[2:3, :] * buf[SUBLANES:SUBLANES + tt, :]
    acc = acc + w_ref[3:4, :] * buf[SUBLANES + 1:SUBLANES + 1 + tt, :]
    o_ref[...] = acc


def _lru_conv(rec, conv_w, conv_b, n_seq):
    b, n_tot, w = rec.shape
    tt = _pick(n_tot - n_seq, (256, 128, 64, 32, 16, 8))
    assert n_seq % tt == 0 and conv_w.shape[0] == 4
    n_lat, n_all = n_seq // tt, n_tot // tt
    r8 = tt // SUBLANES
    last8 = n_tot // SUBLANES - 1
    return pl.pallas_call(
        functools.partial(_lru_conv_kernel, n_lat=n_lat, n_all=n_all),
        grid=(b, n_all),
        in_specs=[pl.BlockSpec((None, tt, w), lambda i, j: (i, j, 0)),
                  pl.BlockSpec((None, SUBLANES, w), lambda i, j: (i, jnp.maximum(j * r8 - 1, 0), 0)),
                  pl.BlockSpec((None, SUBLANES, w), lambda i, j: (i, jnp.minimum((j + 1) * r8, last8), 0)),
                  pl.BlockSpec((4, w), lambda i, j: (0, 0)),
                  pl.BlockSpec((1, w), lambda i, j: (0, 0))],
        out_specs=pl.BlockSpec((None, tt, w), lambda i, j: (i, j, 0)),
        out_shape=jax.ShapeDtypeStruct((b, n_tot, w), F32),
        scratch_shapes=[pltpu.VMEM((tt + 2 * SUBLANES, w), F32)],
        compiler_params=_cparams("parallel", "parallel"),
        name="lru_conv",
    )(rec, rec, rec, conv_w, conv_b.reshape(1, w))


def _lru_scan_kernel(*refs, reverse, combine, blk):
    if combine:
        x_ref, wc_ref, gb_ref, ll_ref, hf_ref, gate_ref, o_ref, a_s, b_s, carry = refs
    else:
        x_ref, wc_ref, gb_ref, ll_ref, o_ref, a_s, b_s, carry = refs
    tt, w = x_ref.shape
    nlb = w // LANES

    @pl.when(pl.program_id(1) == 0)
    def _():
        carry[...] = jnp.zeros_like(carry)

    def put(dst, n, val):
        if blk >= LANES:
            for p in range(blk // LANES):
                dst[n * (blk // LANES) + p] = val[:, p * LANES:(p + 1) * LANES]
        else:
            off = (n * blk) % LANES
            dst[(n * blk) // LANES, :, off:off + blk] = val

    sp = _softplus(-ll_ref[...])
    for n in range(w // blk):
        sl = slice(n * blk, (n + 1) * blk)
        xb = x_ref[:, sl]
        gates = _dot(xb, wc_ref[n])
        rr = _sigmoid(gates[:, :blk] + gb_ref[0:1, sl])
        ii = _sigmoid(gates[:, blk:] + gb_ref[1:2, sl])
        a = jnp.exp(-LRU_C * rr * sp[:, sl])
        put(a_s, n, a)
        put(b_s, n, jnp.sqrt(1.0 - a * a) * (ii * xb))

    groups = tt // SUBLANES
    row = lax.broadcasted_iota(jnp.int32, (groups, SUBLANES, LANES), 1)
    for n in range(nlb):
        a = a_s[n].reshape(groups, SUBLANES, LANES)
        bt = b_s[n].reshape(groups, SUBLANES, LANES)
        for s in (1, 2, 4):
            keep = (row < SUBLANES - s) if reverse else (row >= s)
            shift = SUBLANES - s if reverse else s
            a_sh = jnp.where(keep, pltpu.roll(a, shift, 1), 1.0)
            b_sh = jnp.where(keep, pltpu.roll(bt, shift, 1), 0.0)
            bt = a * b_sh + bt
            a = a * a_sh
        c = carry[n]
        outs = [None] * groups
        for g in (range(groups - 1, -1, -1) if reverse else range(groups)):
            h = a[g] * c + bt[g]
            outs[g] = h
            c = h[0:1, :] if reverse else h[SUBLANES - 1:SUBLANES, :]
        carry[n] = c
        hl = jnp.concatenate(outs, axis=0)
        sl = slice(n * LANES, (n + 1) * LANES)
        if combine:
            o_ref[:, sl] = ((hl + hf_ref[:, sl]) * gate_ref[:, sl]).astype(o_ref.dtype)
        else:
            o_ref[:, sl] = hl


def _lru_scan(x, wcat, gate_b, log_lambda, n_seq, reverse, h_fwd=None, gate=None):
    b, n_tot, w = x.shape
    blk = w // LRU_BLOCKS
    tt = _pick(n_tot - n_seq, (256, 128, 64))
    assert n_seq % tt == 0 and tt % (SUBLANES * SUBLANES) == 0
    n_lat, n_all = n_seq // tt, n_tot // tt
    combine = h_fwd is not None
    tile = lambda i, j: (i, _scan_tile(j, n_lat, n_all, reverse), 0)
    in_specs = [pl.BlockSpec((None, tt, w), tile),
                pl.BlockSpec((LRU_BLOCKS, blk, 2 * blk), lambda i, j: (0, 0, 0)),
                pl.BlockSpec((2, w), lambda i, j: (0, 0)),
                pl.BlockSpec((1, w), lambda i, j: (0, 0))]
    args = [x, wcat, gate_b, log_lambda.reshape(1, w)]
    if combine:
        in_specs += [pl.BlockSpec((None, tt, w), tile), pl.BlockSpec((None, tt, w), tile)]
        args += [h_fwd, gate]
    return pl.pallas_call(
        functools.partial(_lru_scan_kernel, reverse=reverse, combine=combine, blk=blk),
        grid=(b, n_all),
        in_specs=in_specs,
        out_specs=pl.BlockSpec((None, tt, w), tile),
        out_shape=jax.ShapeDtypeStruct((b, n_tot, w), BF16 if combine else F32),
        scratch_shapes=([pltpu.VMEM((w // LANES, tt, LANES), F32)] * 2
                        + [pltpu.VMEM((w // LANES, 1, LANES), F32)]),
        compiler_params=_cparams("parallel", "arbitrary"),
        name="lru_scan_bwd" if reverse else "lru_scan_fwd",
    )(*args)


def _rglru_mixer(us, n_seq, w_in, b_in, conv_w, conv_b, gate_w, gate_b, log_lambda, w_out, b_out):
    b, n_tot, d = us.shape
    w = w_in.shape[1] // 2
    blk = w // LRU_BLOCKS
    x2 = us.reshape(b * n_tot, d)
    w_in = w_in.astype(BF16)
    gate = _mm(x2, w_in, b_in[:w], act="gelu", col0=0, ncols=w).reshape(b, n_tot, w)
    rec = _mm(x2, w_in, b_in[w:], col0=w, ncols=w).reshape(b, n_tot, w)
    x = _lru_conv(rec, conv_w, conv_b, n_seq)
    wcat = jnp.concatenate([gate_w[:, 0], gate_w[:, 1]], axis=-1).astype(BF16)
    h_f = _lru_scan(x, wcat[0], gate_b[0], log_lambda[0], n_seq, False)
    hy = _lru_scan(x, wcat[1], gate_b[1], log_lambda[1], n_seq, True, h_fwd=h_f, gate=gate)
    y = _mm(hy.reshape(b * n_tot, w), w_out.astype(BF16), b_out)
    return y.reshape(b, n_tot, d)


def _gla_kernel(*refs, reverse, combine, scale):
    if combine:
        q_ref, k_ref, v_ref, g_ref, of_ref, o_ref, st = refs
    else:
        q_ref, k_ref, v_ref, g_ref, o_ref, st = refs
    tt = q_ref.shape[0]
    c = GLA_CHUNK

    @pl.when(pl.program_id(2) == 0)
    def _():
        st[...] = jnp.zeros_like(st)

    row = lax.broadcasted_iota(jnp.int32, (c, c), 0)
    col = lax.broadcasted_iota(jnp.int32, (c, c), 1)
    tri = (row <= col) if reverse else (row >= col)
    tri_b = tri.astype(BF16)
    n_chunks = tt // c
    work = []
    for ci in (range(n_chunks - 1, -1, -1) if reverse else range(n_chunks)):
        sl = slice(ci * c, (ci + 1) * c)
        k = k_ref[sl, :]
        v = v_ref[sl, :].astype(BF16)
        bcum = _dot_parts([tri_b], _split(g_ref[sl, :], 3), 3)
        b_last = bcum[0:1, :] if reverse else bcum[c - 1:c, :]
        q_e = (q_ref[sl, :] * scale * jnp.exp(bcum)).astype(BF16)
        k_e = k * jnp.exp(-bcum)
        att = jnp.where(tri, _dot_nt(q_e, k_e), 0.0)
        o = _dot(att, v)
        if combine:
            o = o + of_ref[sl, :]
        kv = _dot_tn(v, k * jnp.exp(b_last - bcum))
        work.append((sl, q_e, o, jnp.exp(b_last), kv))
    s_t = st[...]
    for sl, q_e, o, decay, kv in work:
        o_ref[sl, :] = o + _dot_nt(q_e, s_t)
        s_t = s_t * decay + kv
    st[...] = s_t


def _gla_scan(qkvr, g, n_seq, reverse, o_fwd=None):
    b, n_tot, _ = qkvr.shape
    dk = g.shape[-1]
    hk = dk // GLA_HEADS
    hv = 2 * hk
    tt = _pick(n_tot - n_seq, (256, 128, 64))
    assert n_seq % tt == 0
    n_lat, n_all = n_seq // tt, n_tot // tt
    combine = o_fwd is not None
    kq, kv = dk // hk, 2 * dk // hv
    tile = lambda i, h, j: _scan_tile(j, n_lat, n_all, reverse)
    in_specs = [pl.BlockSpec((None, tt, hk), lambda i, h, j: (i, tile(i, h, j), h)),
                pl.BlockSpec((None, tt, hk), lambda i, h, j: (i, tile(i, h, j), kq + h)),
                pl.BlockSpec((None, tt, hv), lambda i, h, j: (i, tile(i, h, j), kv + h)),
                pl.BlockSpec((None, tt, hk), lambda i, h, j: (i, tile(i, h, j), h))]
    args = [qkvr, qkvr, qkvr, g]
    o_spec = pl.BlockSpec((None, tt, hv), lambda i, h, j: (i, tile(i, h, j), h))
    if combine:
        in_specs.append(o_spec)
        args.append(o_fwd)
    return pl.pallas_call(
        functools.partial(_gla_kernel, reverse=reverse, combine=combine, scale=float(hk) ** -0.5),
        grid=(b, GLA_HEADS, n_all),
        in_specs=in_specs,
        out_specs=o_spec,
        out_shape=jax.ShapeDtypeStruct((b, n_tot, 2 * dk), F32),
        scratch_shapes=[pltpu.VMEM((hv, hk), F32)],
        compiler_params=_cparams("parallel", "parallel", "arbitrary"),
        name="gla_bwd" if reverse else "gla_fwd",
    )(*args)


def _gla_read_kernel(o_ref, r_ref, br_ref, g_ref, y_ref, *, hv):
    for h in range(GLA_HEADS):
        sl = slice(h * hv, (h + 1) * hv)
        o = o_ref[:, sl]
        o = o * lax.rsqrt(jnp.mean(o * o, axis=-1, keepdims=True) + NORM_EPS) * g_ref[:, sl]
        r = r_ref[:, sl] + br_ref[:, sl]
        y_ref[:, sl] = (o * (r * _sigmoid(r))).astype(y_ref.dtype)


def _gla_read(o, qkvr, b_r, norm_g):
    m, dv = o.shape
    tm = _pick(m, (256, 128, 64, 32, 16, 8))
    r_off = qkvr.shape[1] // dv - 1
    return pl.pallas_call(
        functools.partial(_gla_read_kernel, hv=dv // GLA_HEADS),
        grid=(m // tm,),
        in_specs=[pl.BlockSpec((tm, dv), lambda i: (i, 0)),
                  pl.BlockSpec((tm, dv), lambda i: (i, r_off)),
                  pl.BlockSpec((1, dv), lambda i: (0, 0)),
                  pl.BlockSpec((1, dv), lambda i: (0, 0))],
        out_specs=pl.BlockSpec((tm, dv), lambda i: (i, 0)),
        out_shape=jax.ShapeDtypeStruct((m, dv), BF16),
        compiler_params=_cparams("parallel"),
        name="gla_readout",
    )(o, qkvr, b_r.reshape(1, dv), norm_g.reshape(1, dv))


def _gla_mixer(us, n_seq, w_in, b_r, gate_w1, gate_w2, gate_b, norm_g, w_out):
    b, n_tot, d = us.shape
    m = b * n_tot
    x2 = us.reshape(m, d)
    dk = gate_w2.shape[-1]
    rank = gate_w1.shape[-1]
    qkvr = _mm(x2, w_in.astype(BF16))
    low = _mm(x2, jnp.concatenate([gate_w1[0], gate_w1[1]], axis=-1).astype(BF16))
    zeros = jnp.zeros_like(gate_w2[0])
    o = None
    for dr in range(2):
        w2 = jnp.concatenate([gate_w2[0], zeros] if dr == 0 else [zeros, gate_w2[1]], axis=0)
        g = _mm(low, w2.astype(BF16), gate_b[dr], act="gla_gate")
        o = _gla_scan(qkvr.reshape(b, n_tot, -1), g.reshape(b, n_tot, dk), n_seq, dr == 1, o_fwd=o)
    del rank
    hy = _gla_read(o.reshape(m, 2 * dk), qkvr, b_r, norm_g)
    return _mm(hy, w_out.astype(BF16)).reshape(b, n_tot, d)


def _rwkv_mix_kernel(x_ref, p_ref, n_ref, mu_ref, *o_refs, n_lat, n_all):
    t = pl.program_id(1)
    tt = x_ref.shape[0]
    first = jnp.logical_or(t == 0, t == n_lat)
    last = jnp.logical_or(t == n_lat - 1, t == n_all - 1)
    x = x_ref[...]
    prev_row = jnp.where(first, 0.0, p_ref[SUBLANES - 1:SUBLANES, :])
    next_row = jnp.where(last, 0.0, n_ref[0:1, :])
    rid = lax.broadcasted_iota(jnp.int32, x.shape, 0)
    up = jnp.where(rid == 0, prev_row, pltpu.roll(x, 1, 0))
    dn = jnp.where(rid == tt - 1, next_row, pltpu.roll(x, tt - 1, 0))
    dx = 0.5 * (up + dn) - x
    for i, o_ref in enumerate(o_refs):
        o_ref[...] = (x + dx * mu_ref[i:i + 1, :]).astype(o_ref.dtype)


def _rwkv_mix(us, mu, n_seq):
    b, n_tot, d = us.shape
    n_mix = mu.shape[0]
    tt = _pick(n_tot - n_seq, (256, 128, 64, 32, 16, 8))
    assert n_seq % tt == 0
    n_lat, n_all = n_seq // tt, n_tot // tt
    r8 = tt // SUBLANES
    last8 = n_tot // SUBLANES - 1
    out = jax.ShapeDtypeStruct((b, n_tot, d), BF16)
    return pl.pallas_call(
        functools.partial(_rwkv_mix_kernel, n_lat=n_lat, n_all=n_all),
        grid=(b, n_all),
        in_specs=[pl.BlockSpec((None, tt, d), lambda i, j: (i, j, 0)),
                  pl.BlockSpec((None, SUBLANES, d), lambda i, j: (i, jnp.maximum(j * r8 - 1, 0), 0)),
                  pl.BlockSpec((None, SUBLANES, d), lambda i, j: (i, jnp.minimum((j + 1) * r8, last8), 0)),
                  pl.BlockSpec((n_mix, d), lambda i, j: (0, 0))],
        out_specs=[pl.BlockSpec((None, tt, d), lambda i, j: (i, j, 0))] * n_mix,
        out_shape=[out] * n_mix,
        compiler_params=_cparams("parallel", "parallel"),
        name="rwkv_mix",
    )(us, us, us, mu)


def _stack_heads(x, head0):
    return jnp.concatenate([jnp.where(head0, x, 0.0), jnp.where(head0, 0.0, x)], axis=0)


def _split(x, parts):
    out = []
    for _ in range(parts - 1):
        hi = x.astype(BF16)
        out.append(hi)
        x = x - hi.astype(F32)
    out.append(x.astype(BF16))
    return out


def _dot_parts(a_parts, b_parts, order):
    acc = None
    for i, ai in enumerate(a_parts):
        for j, bj in enumerate(b_parts):
            if i + j < order:
                t = jnp.dot(ai, bj, preferred_element_type=F32)
                acc = t if acc is None else acc + t
    return acc


def _dot2(a, b):
    return _dot_parts(_split(a, 2), _split(b, 2), 2)


def _heads_mm(a, b, head0, wide):
    bs = _stack_heads(b, head0)
    return _dot2(a, bs) if wide else _dot(a, bs)


def _rwkv_scan_kernel(r_ref, k_ref, v_ref, a_ref, w_ref, kk_ref, ka_ref, o_ref, z_ref, *, reverse):
    tt = r_ref.shape[0]
    c = RWKV_CHUNK
    lanes = 2 * RWKV_N

    @pl.when(pl.program_id(2) == 0)
    def _():
        z_ref[...] = jnp.zeros_like(z_ref)

    lane = lax.broadcasted_iota(jnp.int32, (1, lanes), 1)
    head0 = lane < RWKV_N
    r2 = lax.broadcasted_iota(jnp.int32, (lanes, lanes), 0)
    c2 = lax.broadcasted_iota(jnp.int32, (lanes, lanes), 1)
    same_head = (r2 < RWKV_N) == (c2 < RWKV_N)
    same_head_f = same_head.astype(F32)
    eye2 = (r2 == c2).astype(F32)
    tr = lax.broadcasted_iota(jnp.int32, (c, c), 0)
    tc = lax.broadcasted_iota(jnp.int32, (c, c), 1)
    tri_f = ((tr <= tc) if reverse else (tr >= tc)).astype(F32)
    pr = lax.broadcasted_iota(jnp.int32, (2 * c, 4 * c), 0)
    pc = lax.broadcasted_iota(jnp.int32, (2 * c, 4 * c), 1) % c
    incl = (pr >= c).astype(jnp.int32)
    pt = pr % c
    pmask = (pt - incl < pc) if reverse else (pt + incl > pc)
    eye_side = (lax.broadcasted_iota(jnp.int32, (c, 2 * c), 0)
                == lax.broadcasted_iota(jnp.int32, (c, 2 * c), 1) % c).astype(F32)
    kk_scale = kk_ref[...]
    ka_scale = ka_ref[...]
    ones_bd = same_head_f.astype(BF16)
    tri_b = tri_f.astype(BF16)
    lvl = (lax.broadcasted_iota(jnp.int32, (c, 2 * c), 0)
           ^ (lax.broadcasted_iota(jnp.int32, (c, 2 * c), 1) % c))

    n_chunks = tt // c
    n_pairs = r_ref.shape[1] // lanes
    order = list(range(n_chunks - 1, -1, -1) if reverse else range(n_chunks))
    sls = [(slice(ci * c, (ci + 1) * c), slice(p * lanes, (p + 1) * lanes))
           for p in range(n_pairs) for ci in order]
    vs = [v_ref[sl] for sl in sls]
    pre = []
    for sl in sls:
        k = k_ref[sl]
        asig = a_ref[sl]
        ld = w_ref[sl]
        kk = k * kk_scale[:, sl[1]]
        sq = _dot_parts(_split(kk * kk, 2), [ones_bd], 2)
        kk = kk * lax.rsqrt(jnp.maximum(sq, 1e-24))
        bvec = kk * asig
        kd = k * (1.0 + (asig - 1.0) * ka_scale[:, sl[1]])
        cum = _dot_parts([tri_b], _split(ld, 3), 3)
        tot = cum[0:1, :] if reverse else cum[c - 1:c, :]
        e_neg = jnp.exp(-cum)
        e_rem = jnp.exp(tot - cum)
        pre.append(dict(at=-kk * jnp.exp(cum - ld), rt=r_ref[sl] * jnp.exp(cum),
                        bt=bvec * e_neg, kt=kd * e_neg, bc=bvec * e_rem, kc=kd * e_rem,
                        wc=jnp.exp(tot)))
    ps = []
    for q in pre:
        lhs = jnp.concatenate([q["at"], q["rt"]], axis=0)
        rhs = jnp.concatenate([_stack_heads(q["bt"], head0), _stack_heads(q["kt"], head0)], axis=0)
        ps.append(jnp.where(pmask, _dot_nt(lhs, rhs), 0.0))
    a_ab = [p[:c, :2 * c] for p in ps]
    a_ak = [p[:c, 2 * c:] for p in ps]
    p_rb = [p[c:, :2 * c] for p in ps]
    p_rk = [p[c:, 2 * c:] for p in ps]
    base = [jnp.where((lvl >> 3) == 0, a, 0.0) for a in a_ab]
    pw2 = [_heads_mm(x, x, head0, True) for x in base]
    pw4 = [_heads_mm(x, x, head0, True) for x in pw2]
    inv = [eye_side + x for x in base]
    inv = [d + _heads_mm(d, x, head0, True) for d, x in zip(inv, pw2)]
    inv = [d + _heads_mm(d, x, head0, True) for d, x in zip(inv, pw4)]
    for bit in (3, 4, 5):
        off = [jnp.where((lvl >> bit) == 1, a, 0.0) for a in a_ab]
        ed = [_heads_mm(e, d, head0, True) for e, d in zip(off, inv)]
        inv = [d + _heads_mm(d, x, head0, True) for d, x in zip(inv, ed)]
    at2 = [_heads_mm(d, q["at"], head0, True) for d, q in zip(inv, pre)]
    av = [_heads_mm(a, v, head0, False) for a, v in zip(a_ak, vs)]
    u_loc = [_heads_mm(d, x, head0, True) for d, x in zip(inv, av)]
    r_eff = [q["rt"] + _heads_mm(p, x, head0, False) for q, p, x in zip(pre, p_rb, at2)]
    y_loc = [_heads_mm(pb, u, head0, False) + _heads_mm(pk, v, head0, False)
             for pb, u, pk, v in zip(p_rb, u_loc, p_rk, vs)]
    m_t = [eye2 * q["wc"] + jnp.where(same_head, _dot_tn(q["bc"], x), 0.0)
           for q, x in zip(pre, at2)]
    n_t = [jnp.where(same_head, _dot_tn(q["bc"], u) + _dot_tn(q["kc"], v), 0.0)
           for q, u, v in zip(pre, u_loc, vs)]
    for p in range(n_pairs):
        z = z_ref[p]
        for i in range(p * n_chunks, (p + 1) * n_chunks):
            o_ref[sls[i]] = _dot(r_eff[i], z) + y_loc[i]
            z = _dot2(m_t[i], z) + n_t[i]
        z_ref[p] = z


def _rwkv_scan(r, k, v, asig, logw, k_k, k_a, n_seq, reverse):
    b, n_tot, d = r.shape
    lanes = 2 * RWKV_N
    tt = _pick(n_tot - n_seq, (256, 128, 64))
    assert n_seq % tt == 0 and d % lanes == 0
    n_lat, n_all = n_seq // tt, n_tot // tt
    n_pairs = 2 if d % (2 * lanes) == 0 else 1
    wide = n_pairs * lanes
    tile = pl.BlockSpec((None, tt, wide),
                        lambda i, h, j: (i, _scan_tile(j, n_lat, n_all, reverse), h))
    vec = pl.BlockSpec((1, wide), lambda i, h, j: (0, h))
    return pl.pallas_call(
        functools.partial(_rwkv_scan_kernel, reverse=reverse),
        grid=(b, d // wide, n_all),
        in_specs=[tile] * 5 + [vec, vec],
        out_specs=tile,
        out_shape=jax.ShapeDtypeStruct((b, n_tot, d), F32),
        scratch_shapes=[pltpu.VMEM((n_pairs, lanes, lanes), F32)],
        compiler_params=_cparams("parallel", "parallel", "arbitrary"),
        name="rwkv_bwd" if reverse else "rwkv_fwd",
    )(r, k, v, asig, logw, k_k.reshape(1, d), k_a.reshape(1, d))


def _rwkv_read_kernel(y0_ref, y1_ref, r_ref, k_ref, v_ref, a0_ref, a1_ref, g_ref,
                      ka_ref, rk_ref, lw_ref, lb_ref, o_ref):
    lanes = 2 * RWKV_N
    r2 = lax.broadcasted_iota(jnp.int32, (lanes, lanes), 0)
    c2 = lax.broadcasted_iota(jnp.int32, (lanes, lanes), 1)
    ones_bd = ((r2 < RWKV_N) == (c2 < RWKV_N)).astype(BF16)
    head_sum = lambda t: _dot_parts(_split(t, 3), [ones_bd], 3)
    y = y0_ref[...] + y1_ref[...]
    mean = head_sum(y) * (1.0 / RWKV_N)
    yc = y - mean
    var = head_sum(yc * yc) * (1.0 / RWKV_N)
    y = yc * lax.rsqrt(var + RWKV_GN_EPS) * lw_ref[...] + lb_ref[...]
    r = r_ref[...]
    k = k_ref[...]
    ka = ka_ref[...]
    k0 = k * (1.0 + (a0_ref[...] - 1.0) * ka)
    k1 = k * (1.0 + (a1_ref[...] - 1.0) * ka)
    rk = rk_ref[...]
    bonus = head_sum(r * k0 * rk + r * k1 * rk) * v_ref[...]
    o_ref[...] = ((y + bonus) * g_ref[...]).astype(o_ref.dtype)


def _rwkv_read(y0, y1, r, k, v, a0, a1, g, k_a, r_k, ln_w, ln_b):
    m, d = r.shape
    lanes = 2 * RWKV_N
    tm = _pick(m, (512, 256, 128, 64, 32, 16, 8))
    tile = pl.BlockSpec((tm, lanes), lambda h, i: (i, h))
    vec = pl.BlockSpec((1, lanes), lambda h, i: (0, h))
    return pl.pallas_call(
        _rwkv_read_kernel,
        grid=(d // lanes, m // tm),
        in_specs=[tile] * 8 + [vec] * 4,
        out_specs=tile,
        out_shape=jax.ShapeDtypeStruct((m, d), BF16),
        compiler_params=_cparams("parallel", "parallel"),
        name="rwkv_readout",
    )(y0, y1, r, k, v, a0, a1, g, k_a.reshape(1, d), r_k.reshape(1, d),
      ln_w.reshape(1, d), ln_b.reshape(1, d))


def _pad_rows(w2, dr):
    zeros = jnp.zeros_like(w2[0])
    return jnp.concatenate([w2[0], zeros] if dr == 0 else [zeros, w2[1]], axis=0).astype(BF16)


def _rwkv_mixer(us, n_seq, mu, w_rkv, w0, w1, w2, a0, a1, a2, g1, g2, k_k, k_a, r_k,
                ln_w, ln_b, w_out):
    b, n_tot, d = us.shape
    m = b * n_tot
    xr, xw, xk, xv, xa, xg = [t.reshape(m, d) for t in _rwkv_mix(us, mu, n_seq)]
    w_rkv = w_rkv.astype(BF16)
    r = _mm(xr, w_rkv[0])
    k = _mm(xk, w_rkv[1])
    v = _mm(xv, w_rkv[2])
    w_low = _mm(xw, jnp.concatenate([w1[0], w1[1]], axis=-1).astype(BF16), act="tanh")
    a_low = _mm(xa, jnp.concatenate([a1[0], a1[1]], axis=-1).astype(BF16))
    g_low = _mm(xg, g1.astype(BF16), act="sigmoid")
    gate = _mm(g_low, g2.astype(BF16))
    sh = (b, n_tot, d)
    ys, asigs = [], []
    for dr in range(2):
        logw = _mm(w_low, _pad_rows(w2, dr), w0[dr], act="rwkv_decay")
        asig = _mm(a_low, _pad_rows(a2, dr), a0[dr], act="sigmoid")
        asigs.append(asig)
        ys.append(_rwkv_scan(r.reshape(sh), k.reshape(sh), v.reshape(sh), asig.reshape(sh),
                             logw.reshape(sh), k_k, k_a, n_seq, dr == 1).reshape(m, d))
    hy = _rwkv_read(ys[0], ys[1], r, k, v, asigs[0], asigs[1], gate, k_a, r_k.reshape(-1),
                    ln_w, ln_b)
    return _mm(hy, w_out.astype(BF16)).reshape(b, n_tot, d)


def kernel(x, c, ctx, c_ctx, ada_w, ada_b, norm_g, ffn_w_in, ffn_w_out, lru_w_in, lru_b_in, lru_conv_w, lru_conv_b, lru_gate_w, lru_gate_b, lru_log_lambda, lru_w_out, lru_b_out, gla_w_in, gla_b_r, gla_gate_w1, gla_gate_w2, gla_gate_b, gla_norm_g, gla_w_out, rwkv_mu, rwkv_w_rkv, rwkv_w0, rwkv_w1, rwkv_w2, rwkv_a0, rwkv_a1, rwkv_a2, rwkv_g1, rwkv_g2, rwkv_k_k, rwkv_k_a, rwkv_r_k, rwkv_ln_w, rwkv_ln_b, rwkv_w_out):
    b, n_seq, d = x.shape
    depth = ada_w.shape[0]
    assert b < 8
    cond = jnp.concatenate([c, c_ctx[None, :], jnp.zeros((8 - b - 1, d), F32)], axis=0)
    mods = _mods(cond, ada_w, ada_b).reshape(depth, 8, N_MOD, d)
    h_lat, h_ctx = x, ctx
    for i in range(depth):
        ctx_out = i < depth - 1
        kind, j = i % N_MIXERS, i // N_MIXERS
        col_major = i % 2 == 1
        m_lat, m_ctx = mods[i, :b], mods[i, b]
        us = _norm_mod(h_lat, h_ctx, norm_g[i, 0], m_lat, m_ctx, 0, 1, col_major,
                       F32 if kind == 2 else BF16)
        if kind == 0:
            y = _rglru_mixer(us, n_seq, lru_w_in[j], lru_b_in[j], lru_conv_w[j], lru_conv_b[j],
                             lru_gate_w[j], lru_gate_b[j], lru_log_lambda[j], lru_w_out[j],
                             lru_b_out[j])
        elif kind == 1:
            y = _gla_mixer(us, n_seq, gla_w_in[j], gla_b_r[j], gla_gate_w1[j], gla_gate_w2[j],
                           gla_gate_b[j], gla_norm_g[j], gla_w_out[j])
        else:
            y = _rwkv_mixer(us, n_seq, rwkv_mu[j], rwkv_w_rkv[j], rwkv_w0[j], rwkv_w1[j],
                            rwkv_w2[j], rwkv_a0[j], rwkv_a1[j], rwkv_a2[j], rwkv_g1[j],
                            rwkv_g2[j], rwkv_k_k[j], rwkv_k_a[j], rwkv_r_k[j], rwkv_ln_w[j],
                            rwkv_ln_b[j], rwkv_w_out[j])
        h_lat, h_ctx = _resid(h_lat, h_ctx if ctx_out else None, y, norm_g[i, 1], m_lat, m_ctx,
                              2, col_major)
        us = _norm_mod(h_lat, h_ctx, norm_g[i, 2], m_lat, m_ctx, 3, 4, False, BF16)
        n_tot = us.shape[1]
        mid = _ffn_in(us.reshape(b * n_tot, d), ffn_w_in[i].astype(BF16))
        y = _mm(mid, ffn_w_out[i].astype(BF16)).reshape(b, n_tot, d)
        h_lat, h_ctx = _resid(h_lat, h_ctx, y, norm_g[i, 3], m_lat, m_ctx, 5, False)
    return h_lat
```

```python
import functools

import jax
import jax.numpy as jnp
from jax import lax
from jax.experimental import pallas as pl
from jax.experimental.pallas import tpu as pltpu

F32 = jnp.float32
BF16 = jnp.bfloat16

GRID_W = 64
N_MIXERS = 3
N_MOD = 6
NORM_EPS = 1e-6
LRU_BLOCKS = 16
LRU_C = 8.0
GLA_HEADS = 4
GLA_TAU = 16.0
GLA_CHUNK = 64
RWKV_N = 64
RWKV_CHUNK = 64
RWKV_GN_EPS = 64e-5
LANES = 128
SUBLANES = 8
VMEM_LIMIT = 52 * 1024 * 1024


def _cparams(*sem):
    return pltpu.CompilerParams(dimension_semantics=sem, vmem_limit_bytes=VMEM_LIMIT)


def _pick(n, cands):
    for c in cands:
        if n % c == 0:
            return c
    return n


def _dot(a, b):
    return jnp.dot(a.astype(BF16), b.astype(BF16), preferred_element_type=F32)


def _dot_nt(a, b):
    return lax.dot_general(a.astype(BF16), b.astype(BF16), (((1,), (1,)), ((), ())),
                           preferred_element_type=F32)


def _dot_tn(a, b):
    return lax.dot_general(a.astype(BF16), b.astype(BF16), (((0,), (0,)), ((), ())),
                           preferred_element_type=F32)


def _sigmoid(x):
    return 1.0 / (1.0 + jnp.exp(-x))


def _softplus(x):
    return jnp.maximum(x, 0.0) + jnp.log(1.0 + jnp.exp(-jnp.abs(x)))


def _mods_kernel(x_ref, w_ref, b_ref, o_ref):
    x = x_ref[...]
    x = x * _sigmoid(x)
    o_ref[...] = _dot(x, w_ref[...]) + b_ref[...]


def _mods(cond, ada_w, ada_b):
    n_layers, d, n = ada_w.shape
    tn = _pick(n, (1024, 512, 256, 128))
    return pl.pallas_call(
        _mods_kernel,
        grid=(n_layers, n // tn),
        in_specs=[pl.BlockSpec((8, d), lambda l, j: (0, 0)),
                  pl.BlockSpec((None, d, tn), lambda l, j: (l, 0, j)),
                  pl.BlockSpec((None, 1, tn), lambda l, j: (l, 0, j))],
        out_specs=pl.BlockSpec((None, 8, tn), lambda l, j: (l, 0, j)),
        out_shape=jax.ShapeDtypeStruct((n_layers, 8, n), F32),
        compiler_params=_cparams("parallel", "parallel"),
        name="ada_mods",
    )(cond, ada_w, ada_b.reshape(n_layers, 1, n))


def _norm_mod_kernel(*refs, k_shift, k_scale, n_lat):
    def emit(h_ref, m_ref, o_ref):
        x = h_ref[...]
        y = x * lax.rsqrt(jnp.mean(x * x, axis=-1, keepdims=True) + NORM_EPS) * g_ref[...]
        y = y * (1.0 + m_ref[k_scale:k_scale + 1, :]) + m_ref[k_shift:k_shift + 1, :]
        o_ref[...] = y.astype(o_ref.dtype)

    if n_lat is None:
        h_ref, g_ref, m_ref, o_ref = refs
        emit(h_ref, m_ref, o_ref)
        return
    h_ref, c_ref, g_ref, m_ref, mc_ref, o_ref = refs
    is_ctx = pl.program_id(1) >= n_lat

    @pl.when(jnp.logical_not(is_ctx))
    def _():
        emit(h_ref, m_ref, o_ref)

    @pl.when(is_ctx)
    def _():
        emit(c_ref, mc_ref, o_ref)


def _norm_mod(h_lat, h_ctx, g, m_lat, m_ctx, k_shift, k_scale, col_major, out_dtype):
    b, n, d = h_lat.shape
    rows = n // GRID_W
    n_ctx = 0 if h_ctx is None else h_ctx.shape[1]
    g = g.reshape(1, d)
    if col_major:
        tt = rows
        n_lat = GRID_W
        h_in = h_lat.reshape(b, rows, GRID_W * d)
        h_spec = pl.BlockSpec((None, rows, d), lambda i, j: (i, 0, jnp.minimum(j, n_lat - 1)))
    else:
        tt = _pick(n_ctx if n_ctx else n, (256, 128, 64, 32, 16, 8))
        n_lat = n // tt
        h_in = h_lat
        h_spec = pl.BlockSpec((None, tt, d), lambda i, j: (i, jnp.minimum(j, n_lat - 1), 0))
    assert n_ctx % tt == 0 and n % tt == 0
    g_spec = pl.BlockSpec((1, d), lambda i, j: (0, 0))
    m_spec = pl.BlockSpec((None, N_MOD, d), lambda i, j: (i, 0, 0))
    if h_ctx is None:
        in_specs, args = [h_spec, g_spec, m_spec], (h_in, g, m_lat)
    else:
        in_specs = [h_spec,
                    pl.BlockSpec((None, tt, d), lambda i, j: (i, jnp.maximum(j - n_lat, 0), 0)),
                    g_spec, m_spec,
                    pl.BlockSpec((None, N_MOD, d), lambda i, j: (0, 0, 0))]
        args = (h_in, h_ctx, g, m_lat, m_ctx.reshape(1, N_MOD, d))
    return pl.pallas_call(
        functools.partial(_norm_mod_kernel, k_shift=k_shift, k_scale=k_scale,
                          n_lat=None if h_ctx is None else n_lat),
        grid=(b, n_lat + n_ctx // tt),
        in_specs=in_specs,
        out_specs=pl.BlockSpec((None, tt, d), lambda i, j: (i, j, 0)),
        out_shape=jax.ShapeDtypeStruct((b, n + n_ctx, d), out_dtype),
        compiler_params=_cparams("parallel", "arbitrary"),
        name="norm_mod",
    )(*args)


def _resid_kernel(h_ref, y_ref, g_ref, m_ref, o_ref, *, k_gate):
    y = y_ref[...]
    yn = y * lax.rsqrt(jnp.mean(y * y, axis=-1, keepdims=True) + NORM_EPS) * g_ref[...]
    o_ref[...] = h_ref[...] + m_ref[k_gate:k_gate + 1, :] * yn


def _resid(h_lat, h_ctx, y, g, m_lat, m_ctx, k_gate, col_major):
    b, n, d = h_lat.shape
    rows = n // GRID_W
    g = g.reshape(1, d)
    kern = functools.partial(_resid_kernel, k_gate=k_gate)
    if col_major:
        tt = rows
        grid = (b, GRID_W)
        h_in = h_lat.reshape(b, rows, GRID_W * d)
        h_spec = pl.BlockSpec((None, rows, d), lambda i, j: (i, 0, j))
        out_shape = jax.ShapeDtypeStruct((b, rows, GRID_W * d), F32)
    else:
        tt = _pick(n, (256, 128, 64, 32, 16, 8))
        grid = (b, n // tt)
        h_in = h_lat
        h_spec = pl.BlockSpec((None, tt, d), lambda i, j: (i, j, 0))
        out_shape = jax.ShapeDtypeStruct((b, n, d), F32)
    new_lat = pl.pallas_call(
        kern,
        grid=grid,
        in_specs=[h_spec,
                  pl.BlockSpec((None, tt, d), lambda i, j: (i, j, 0)),
                  pl.BlockSpec((1, d), lambda i, j: (0, 0)),
                  pl.BlockSpec((None, N_MOD, d), lambda i, j: (i, 0, 0))],
        out_specs=h_spec,
        out_shape=out_shape,
        compiler_params=_cparams("parallel", "parallel"),
        name="resid_lat",
    )(h_in, y, g, m_lat).reshape(b, n, d)
    if h_ctx is None:
        return new_lat, None
    n_ctx = h_ctx.shape[1]
    tc = _pick(n_ctx, (256, 128, 64, 32, 16, 8))
    off = n // tc
    new_ctx = pl.pallas_call(
        kern,
        grid=(b, n_ctx // tc),
        in_specs=[pl.BlockSpec((None, tc, d), lambda i, j: (i, j, 0)),
                  pl.BlockSpec((None, tc, d), lambda i, j: (i, off + j, 0)),
                  pl.BlockSpec((1, d), lambda i, j: (0, 0)),
                  pl.BlockSpec((None, N_MOD, d), lambda i, j: (0, 0, 0))],
        out_specs=pl.BlockSpec((None, tc, d), lambda i, j: (i, j, 0)),
        out_shape=jax.ShapeDtypeStruct((b, n_ctx, d), F32),
        compiler_params=_cparams("parallel", "parallel"),
        name="resid_ctx",
    )(h_ctx, y, g, m_ctx.reshape(1, N_MOD, d))
    return new_lat, new_ctx


def _gelu_tanh(x):
    return 0.5 * x * (1.0 + jnp.tanh(0.7978845608028654 * (x + 0.044715 * x * x * x)))


def _apply_act(z, act):
    if act is None:
        return z
    if act == "gelu":
        return _gelu_tanh(z)
    if act == "tanh":
        return jnp.tanh(z)
    if act == "sigmoid":
        return _sigmoid(z)
    if act == "gla_gate":
        return -_softplus(-z) * (1.0 / GLA_TAU)
    if act == "rwkv_decay":
        return -jnp.exp(-_softplus(-z) - 0.5)
    raise ValueError(act)


def _mm_kernel(x_ref, w_ref, b_ref, o_ref, wb, *, act):
    @pl.when(pl.program_id(1) == 0)
    def _():
        wb[...] = w_ref[...].astype(BF16)

    z = _dot(x_ref[...], wb[...]) + b_ref[...]
    o_ref[...] = _apply_act(z, act).astype(o_ref.dtype)


def _mm(x, w, bias=None, act=None, out_dtype=F32, col0=0, ncols=None, widx=()):
    m, k = x.shape
    n_total = w.shape[-1]
    ncols = n_total if ncols is None else ncols
    tm = _pick(m, (512, 256, 128, 64, 32, 16, 8))
    tn = _pick(ncols, (1024, 512, 256, 128)) if k <= 2048 else _pick(ncols, (512, 256, 128))
    assert col0 % tn == 0 and w.ndim == 2 + len(widx)
    off = col0 // tn
    if bias is None:
        bias = jnp.zeros((ncols,), F32)
    return pl.pallas_call(
        functools.partial(_mm_kernel, act=act),
        grid=(ncols // tn, m // tm),
        in_specs=[pl.BlockSpec((tm, k), lambda j, i: (i, 0)),
                  pl.BlockSpec((None,) * len(widx) + (k, tn), lambda j, i: widx + (0, off + j)),
                  pl.BlockSpec((1, tn), lambda j, i: (0, j))],
        out_specs=pl.BlockSpec((tm, tn), lambda j, i: (i, j)),
        out_shape=jax.ShapeDtypeStruct((m, ncols), out_dtype),
        scratch_shapes=[pltpu.VMEM((k, tn), BF16)],
        compiler_params=_cparams("parallel", "arbitrary"),
        name="proj",
    )(x, w, bias.reshape(1, ncols).astype(F32))


def _ffn_in_kernel(x_ref, wg_ref, wu_ref, o_ref, wgb, wub):
    @pl.when(pl.program_id(1) == 0)
    def _():
        wgb[...] = wg_ref[...].astype(BF16)
        wub[...] = wu_ref[...].astype(BF16)

    x = x_ref[...]
    gate = _dot(x, wgb[...])
    up = _dot(x, wub[...])
    o_ref[...] = (gate * _sigmoid(gate) * up).astype(o_ref.dtype)


def _ffn_in(x, w_in, layer):
    m, k = x.shape
    d_ff = w_in.shape[-1] // 2
    tm = _pick(m, (512, 256, 128, 64, 32, 16, 8))
    tn = _pick(d_ff, (512, 256, 128))
    off = d_ff // tn
    return pl.pallas_call(
        _ffn_in_kernel,
        grid=(d_ff // tn, m // tm),
        in_specs=[pl.BlockSpec((tm, k), lambda j, i: (i, 0)),
                  pl.BlockSpec((None, k, tn), lambda j, i: (layer, 0, j)),
                  pl.BlockSpec((None, k, tn), lambda j, i: (layer, 0, off + j))],
        out_specs=pl.BlockSpec((tm, tn), lambda j, i: (i, j)),
        out_shape=jax.ShapeDtypeStruct((m, d_ff), BF16),
        scratch_shapes=[pltpu.VMEM((k, tn), BF16)] * 2,
        compiler_params=_cparams("parallel", "arbitrary"),
        name="ffn_in",
    )(x, w_in, w_in)


def _scan_tile(step, n_lat, n_all, reverse):
    if reverse:
        return n_all - 1 - step
    n_ctx = n_all - n_lat
    return jnp.where(step < n_ctx, n_lat + step, step - n_ctx)


def _lru_conv_kernel(x_ref, p_ref, n_ref, w_ref, b_ref, o_ref, buf, *, n_lat, n_all):
    t = pl.program_id(1)
    tt = x_ref.shape[0]
    first = jnp.logical_or(t == 0, t == n_lat)
    last = jnp.logical_or(t == n_lat - 1, t == n_all - 1)
    buf[0:SUBLANES, :] = jnp.where(first, 0.0, p_ref[...])
    buf[SUBLANES:SUBLANES + tt, :] = x_ref[...]
    buf[SUBLANES + tt:2 * SUBLANES + tt, :] = jnp.where(last, 0.0, n_ref[...])
    acc = b_ref[...] + w_ref[0:1, :] * buf[SUBLANES - 2:SUBLANES - 2 + tt, :]
    acc = acc + w_ref[1:2, :] * buf[SUBLANES - 1:SUBLANES - 1 + tt, :]
    acc = acc + w_ref[2:3, :] * buf[SUBLANES:SUBLANES + tt, :]
    acc = acc + w_ref[3:4, :] * buf[SUBLANES + 1:SUBLANES + 1 + tt, :]
    o_ref[...] = acc


def _lru_conv(rec, conv_w, conv_b, n_seq):
    b, n_tot, w = rec.shape
    tt = _pick(n_tot - n_seq, (256, 128, 64, 32, 16, 8))
    assert n_seq % tt == 0 and conv_w.shape[0] == 4
    n_lat, n_all = n_seq // tt, n_tot // tt
    r8 = tt // SUBLANES
    last8 = n_tot // SUBLANES - 1
    return pl.pallas_call(
        functools.partial(_lru_conv_kernel, n_lat=n_lat, n_all=n_all),
        grid=(b, n_all),
        in_specs=[pl.BlockSpec((None, tt, w), lambda i, j: (i, j, 0)),
                  pl.BlockSpec((None, SUBLANES, w), lambda i, j: (i, jnp.maximum(j * r8 - 1, 0), 0)),
                  pl.BlockSpec((None, SUBLANES, w), lambda i, j: (i, jnp.minimum((j + 1) * r8, last8), 0)),
                  pl.BlockSpec((4, w), lambda i, j: (0, 0)),
                  pl.BlockSpec((1, w), lambda i, j: (0, 0))],
        out_specs=pl.BlockSpec((None, tt, w), lambda i, j: (i, j, 0)),
        out_shape=jax.ShapeDtypeStruct((b, n_tot, w), F32),
        scratch_shapes=[pltpu.VMEM((tt + 2 * SUBLANES, w), F32)],
        compiler_params=_cparams("parallel", "parallel"),
        name="lru_conv",
    )(rec, rec, rec, conv_w, conv_b.reshape(1, w))


def _lru_scan_kernel(*refs, reverse, combine, blk):
    if combine:
        x_ref, wc_ref, gb_ref, ll_ref, hf_ref, gate_ref, o_ref, a_s, b_s, carry = refs
    else:
        x_ref, wc_ref, gb_ref, ll_ref, o_ref, a_s, b_s, carry = refs
    tt, w = x_ref.shape
    nlb = w // LANES

    @pl.when(pl.program_id(1) == 0)
    def _():
        carry[...] = jnp.zeros_like(carry)

    def put(dst, n, val):
        if blk >= LANES:
            for p in range(blk // LANES):
                dst[n * (blk // LANES) + p] = val[:, p * LANES:(p + 1) * LANES]
        else:
            off = (n * blk) % LANES
            dst[(n * blk) // LANES, :, off:off + blk] = val

    sp = _softplus(-ll_ref[...])
    for n in range(w // blk):
        sl = slice(n * blk, (n + 1) * blk)
        xb = x_ref[:, sl]
        gates = _dot(xb, wc_ref[n])
        rr = _sigmoid(gates[:, :blk] + gb_ref[0:1, sl])
        ii = _sigmoid(gates[:, blk:] + gb_ref[1:2, sl])
        a = jnp.exp(-LRU_C * rr * sp[:, sl])
        put(a_s, n, a)
        put(b_s, n, jnp.sqrt(1.0 - a * a) * (ii * xb))

    groups = tt // SUBLANES
    row = lax.broadcasted_iota(jnp.int32, (groups, SUBLANES, LANES), 1)
    for n in range(nlb):
        a = a_s[n].reshape(groups, SUBLANES, LANES)
        bt = b_s[n].reshape(groups, SUBLANES, LANES)
        for s in (1, 2, 4):
            keep = (row < SUBLANES - s) if reverse else (row >= s)
            shift = SUBLANES - s if reverse else s
            a_sh = jnp.where(keep, pltpu.roll(a, shift, 1), 1.0)
            b_sh = jnp.where(keep, pltpu.roll(bt, shift, 1), 0.0)
            bt = a * b_sh + bt
            a = a * a_sh
        c = carry[n]
        outs = [None] * groups
        for g in (range(groups - 1, -1, -1) if reverse else range(groups)):
            h = a[g] * c + bt[g]
            outs[g] = h
            c = h[0:1, :] if reverse else h[SUBLANES - 1:SUBLANES, :]
        carry[n] = c
        hl = jnp.concatenate(outs, axis=0)
        sl = slice(n * LANES, (n + 1) * LANES)
        if combine:
            o_ref[:, sl] = ((hl + hf_ref[:, sl]) * gate_ref[:, sl]).astype(o_ref.dtype)
        else:
            o_ref[:, sl] = hl


def _lru_scan(x, wcat, gate_b, log_lambda, n_seq, reverse, h_fwd=None, gate=None):
    b, n_tot, w = x.shape
    blk = w // LRU_BLOCKS
    tt = _pick(n_tot - n_seq, (256, 128, 64))
    assert n_seq % tt == 0 and tt % (SUBLANES * SUBLANES) == 0
    n_lat, n_all = n_seq // tt, n_tot // tt
    combine = h_fwd is not None
    tile = lambda i, j: (i, _scan_tile(j, n_lat, n_all, reverse), 0)
    in_specs = [pl.BlockSpec((None, tt, w), tile),
                pl.BlockSpec((LRU_BLOCKS, blk, 2 * blk), lambda i, j: (0, 0, 0)),
                pl.BlockSpec((2, w), lambda i, j: (0, 0)),
                pl.BlockSpec((1, w), lambda i, j: (0, 0))]
    args = [x, wcat, gate_b, log_lambda.reshape(1, w)]
    if combine:
        in_specs += [pl.BlockSpec((None, tt, w), tile), pl.BlockSpec((None, tt, w), tile)]
        args += [h_fwd, gate]
    return pl.pallas_call(
        functools.partial(_lru_scan_kernel, reverse=reverse, combine=combine, blk=blk),
        grid=(b, n_all),
        in_specs=in_specs,
        out_specs=pl.BlockSpec((None, tt, w), tile),
        out_shape=jax.ShapeDtypeStruct((b, n_tot, w), BF16 if combine else F32),
        scratch_shapes=([pltpu.VMEM((w // LANES, tt, LANES), F32)] * 2
                        + [pltpu.VMEM((w // LANES, 1, LANES), F32)]),
        compiler_params=_cparams("parallel", "arbitrary"),
        name="lru_scan_bwd" if reverse else "lru_scan_fwd",
    )(*args)


def _rglru_mixer(us, n_seq, j, w_in, b_in, conv_w, conv_b, gate_w, gate_b, log_lambda, w_out, b_out):
    b, n_tot, d = us.shape
    w = w_in.shape[-1] // 2
    x2 = us.reshape(b * n_tot, d)
    gate = _mm(x2, w_in, b_in[:w], act="gelu", col0=0, ncols=w, widx=(j,)).reshape(b, n_tot, w)
    rec = _mm(x2, w_in, b_in[w:], col0=w, ncols=w, widx=(j,)).reshape(b, n_tot, w)
    x = _lru_conv(rec, conv_w, conv_b, n_seq)
    wcat = jnp.concatenate([gate_w[:, 0], gate_w[:, 1]], axis=-1).astype(BF16)
    h_f = _lru_scan(x, wcat[0], gate_b[0], log_lambda[0], n_seq, False)
    hy = _lru_scan(x, wcat[1], gate_b[1], log_lambda[1], n_seq, True, h_fwd=h_f, gate=gate)
    y = _mm(hy.reshape(b * n_tot, w), w_out, b_out, widx=(j,))
    return y.reshape(b, n_tot, d)


def _gla_kernel(*refs, reverse, combine, scale):
    if combine:
        q_ref, k_ref, v_ref, g_ref, of_ref, o_ref, st = refs
    else:
        q_ref, k_ref, v_ref, g_ref, o_ref, st = refs
    tt = q_ref.shape[0]
    c = GLA_CHUNK

    @pl.when(pl.program_id(2) == 0)
    def _():
        st[...] = jnp.zeros_like(st)

    row = lax.broadcasted_iota(jnp.int32, (c, c), 0)
    col = lax.broadcasted_iota(jnp.int32, (c, c), 1)
    tri = (row <= col) if reverse else (row >= col)
    tri_b = tri.astype(BF16)
    n_chunks = tt // c
    n_heads, hv, hk = st.shape
    work = []
    for p in range(n_heads):
        lk = slice(p * hk, (p + 1) * hk)
        lv = slice(p * hv, (p + 1) * hv)
        for ci in (range(n_chunks - 1, -1, -1) if reverse else range(n_chunks)):
            sl = slice(ci * c, (ci + 1) * c)
            k = k_ref[sl, lk]
            v = v_ref[sl, lv].astype(BF16)
            bcum = _dot_parts([tri_b], _split(g_ref[sl, lk], 3), 3)
            b_last = bcum[0:1, :] if reverse else bcum[c - 1:c, :]
            q_e = (q_ref[sl, lk] * scale * jnp.exp(bcum)).astype(BF16)
            k_e = k * jnp.exp(-bcum)
            att = jnp.where(tri, _dot_nt(q_e, k_e), 0.0)
            o = _dot(att, v)
            if combine:
                o = o + of_ref[sl, lv]
            kv = _dot_tn(v, k * jnp.exp(b_last - bcum))
            work.append((sl, lv, q_e, o, jnp.exp(b_last), kv))
    for p in range(n_heads):
        s_t = st[p]
        for sl, lv, q_e, o, decay, kv in work[p * n_chunks:(p + 1) * n_chunks]:
            o_ref[sl, lv] = o + _dot_nt(q_e, s_t)
            s_t = s_t * decay + kv
        st[p] = s_t


def _gla_scan(qkvr, g, n_seq, reverse, o_fwd=None):
    b, n_tot, _ = qkvr.shape
    dk = g.shape[-1]
    hk = dk // GLA_HEADS
    hv = 2 * hk
    tt = _pick(n_tot - n_seq, (256, 128, 64))
    assert n_seq % tt == 0
    n_lat, n_all = n_seq // tt, n_tot // tt
    combine = o_fwd is not None
    hps = 2
    wk, wv = hps * hk, hps * hv
    kq, kv = dk // wk, 2 * dk // wv
    tile = lambda i, h, j: _scan_tile(j, n_lat, n_all, reverse)
    in_specs = [pl.BlockSpec((None, tt, wk), lambda i, h, j: (i, tile(i, h, j), h)),
                pl.BlockSpec((None, tt, wk), lambda i, h, j: (i, tile(i, h, j), kq + h)),
                pl.BlockSpec((None, tt, wv), lambda i, h, j: (i, tile(i, h, j), kv + h)),
                pl.BlockSpec((None, tt, wk), lambda i, h, j: (i, tile(i, h, j), h))]
    args = [qkvr, qkvr, qkvr, g]
    o_spec = pl.BlockSpec((None, tt, wv), lambda i, h, j: (i, tile(i, h, j), h))
    if combine:
        in_specs.append(o_spec)
        args.append(o_fwd)
    return pl.pallas_call(
        functools.partial(_gla_kernel, reverse=reverse, combine=combine, scale=float(hk) ** -0.5),
        grid=(b, GLA_HEADS // hps, n_all),
        in_specs=in_specs,
        out_specs=o_spec,
        out_shape=jax.ShapeDtypeStruct((b, n_tot, 2 * dk), F32),
        scratch_shapes=[pltpu.VMEM((hps, hv, hk), F32)],
        compiler_params=_cparams("parallel", "parallel", "arbitrary"),
        name="gla_bwd" if reverse else "gla_fwd",
    )(*args)


def _gla_read_kernel(o_ref, r_ref, br_ref, g_ref, y_ref, *, hv):
    for h in range(GLA_HEADS):
        sl = slice(h * hv, (h + 1) * hv)
        o = o_ref[:, sl]
        o = o * lax.rsqrt(jnp.mean(o * o, axis=-1, keepdims=True) + NORM_EPS) * g_ref[:, sl]
        r = r_ref[:, sl] + br_ref[:, sl]
        y_ref[:, sl] = (o * (r * _sigmoid(r))).astype(y_ref.dtype)


def _gla_read(o, qkvr, b_r, norm_g):
    m, dv = o.shape
    tm = _pick(m, (256, 128, 64, 32, 16, 8))
    r_off = qkvr.shape[1] // dv - 1
    return pl.pallas_call(
        functools.partial(_gla_read_kernel, hv=dv // GLA_HEADS),
        grid=(m // tm,),
        in_specs=[pl.BlockSpec((tm, dv), lambda i: (i, 0)),
                  pl.BlockSpec((tm, dv), lambda i: (i, r_off)),
                  pl.BlockSpec((1, dv), lambda i: (0, 0)),
                  pl.BlockSpec((1, dv), lambda i: (0, 0))],
        out_specs=pl.BlockSpec((tm, dv), lambda i: (i, 0)),
        out_shape=jax.ShapeDtypeStruct((m, dv), BF16),
        compiler_params=_cparams("parallel"),
        name="gla_readout",
    )(o, qkvr, b_r.reshape(1, dv), norm_g.reshape(1, dv))


def _gla_mixer(us, n_seq, j, w_in, b_r, gate_w1, gate_w2, gate_b, norm_g, w_out):
    b, n_tot, d = us.shape
    m = b * n_tot
    x2 = us.reshape(m, d)
    dk = gate_w2.shape[-1]
    qkvr = _mm(x2, w_in, widx=(j,))
    low = _mm(x2, jnp.concatenate([gate_w1[0], gate_w1[1]], axis=-1))
    zeros = jnp.zeros_like(gate_w2[0])
    o = None
    for dr in range(2):
        w2 = jnp.concatenate([gate_w2[0], zeros] if dr == 0 else [zeros, gate_w2[1]], axis=0)
        g = _mm(low, w2, gate_b[dr], act="gla_gate")
        o = _gla_scan(qkvr.reshape(b, n_tot, -1), g.reshape(b, n_tot, dk), n_seq, dr == 1, o_fwd=o)
    hy = _gla_read(o.reshape(m, 2 * dk), qkvr, b_r, norm_g)
    return _mm(hy, w_out, widx=(j,)).reshape(b, n_tot, d)


def _rwkv_mix_kernel(x_ref, p_ref, n_ref, mu_ref, *o_refs, n_lat, n_all):
    t = pl.program_id(1)
    tt = x_ref.shape[0]
    first = jnp.logical_or(t == 0, t == n_lat)
    last = jnp.logical_or(t == n_lat - 1, t == n_all - 1)
    x = x_ref[...]
    prev_row = jnp.where(first, 0.0, p_ref[SUBLANES - 1:SUBLANES, :])
    next_row = jnp.where(last, 0.0, n_ref[0:1, :])
    rid = lax.broadcasted_iota(jnp.int32, x.shape, 0)
    up = jnp.where(rid == 0, prev_row, pltpu.roll(x, 1, 0))
    dn = jnp.where(rid == tt - 1, next_row, pltpu.roll(x, tt - 1, 0))
    dx = 0.5 * (up + dn) - x
    for i, o_ref in enumerate(o_refs):
        o_ref[...] = (x + dx * mu_ref[i:i + 1, :]).astype(o_ref.dtype)


def _rwkv_mix(us, mu, n_seq):
    b, n_tot, d = us.shape
    n_mix = mu.shape[0]
    tt = _pick(n_tot - n_seq, (256, 128, 64, 32, 16, 8))
    assert n_seq % tt == 0
    n_lat, n_all = n_seq // tt, n_tot // tt
    r8 = tt // SUBLANES
    last8 = n_tot // SUBLANES - 1
    out = jax.ShapeDtypeStruct((b, n_tot, d), BF16)
    return pl.pallas_call(
        functools.partial(_rwkv_mix_kernel, n_lat=n_lat, n_all=n_all),
        grid=(b, n_all),
        in_specs=[pl.BlockSpec((None, tt, d), lambda i, j: (i, j, 0)),
                  pl.BlockSpec((None, SUBLANES, d), lambda i, j: (i, jnp.maximum(j * r8 - 1, 0), 0)),
                  pl.BlockSpec((None, SUBLANES, d), lambda i, j: (i, jnp.minimum((j + 1) * r8, last8), 0)),
                  pl.BlockSpec((n_mix, d), lambda i, j: (0, 0))],
        out_specs=[pl.BlockSpec((None, tt, d), lambda i, j: (i, j, 0))] * n_mix,
        out_shape=[out] * n_mix,
        compiler_params=_cparams("parallel", "parallel"),
        name="rwkv_mix",
    )(us, us, us, mu)


def _stack_heads(x, head0):
    return jnp.concatenate([jnp.where(head0, x, 0.0), jnp.where(head0, 0.0, x)], axis=0)


def _split(x, parts):
    out = []
    for _ in range(parts - 1):
        hi = x.astype(BF16)
        out.append(hi)
        x = x - hi.astype(F32)
    out.append(x.astype(BF16))
    return out


def _dot_parts(a_parts, b_parts, order):
    acc = None
    for i, ai in enumerate(a_parts):
        for j, bj in enumerate(b_parts):
            if i + j < order:
                t = jnp.dot(ai, bj, preferred_element_type=F32)
                acc = t if acc is None else acc + t
    return acc


def _dot2(a, b):
    return _dot_parts(_split(a, 2), _split(b, 2), 2)


def _heads_mm(a, b, head0, wide):
    bs = _stack_heads(b, head0)
    return _dot2(a, bs) if wide else _dot(a, bs)


def _rwkv_scan_kernel(r_ref, k_ref, v_ref, a_ref, w_ref, kk_ref, ka_ref, o_ref, z_ref, *, reverse):
    tt = r_ref.shape[0]
    c = RWKV_CHUNK
    lanes = 2 * RWKV_N

    @pl.when(pl.program_id(2) == 0)
    def _():
        z_ref[...] = jnp.zeros_like(z_ref)

    lane = lax.broadcasted_iota(jnp.int32, (1, lanes), 1)
    head0 = lane < RWKV_N
    r2 = lax.broadcasted_iota(jnp.int32, (lanes, lanes), 0)
    c2 = lax.broadcasted_iota(jnp.int32, (lanes, lanes), 1)
    same_head = (r2 < RWKV_N) == (c2 < RWKV_N)
    eye2 = (r2 == c2).astype(F32)
    tr = lax.broadcasted_iota(jnp.int32, (c, c), 0)
    tc = lax.broadcasted_iota(jnp.int32, (c, c), 1)
    tri_b = ((tr <= tc) if reverse else (tr >= tc)).astype(BF16)
    pr = lax.broadcasted_iota(jnp.int32, (2 * c, 4 * c), 0)
    pc = lax.broadcasted_iota(jnp.int32, (2 * c, 4 * c), 1) % c
    incl = (pr >= c).astype(jnp.int32)
    pt = pr % c
    pmask = (pt - incl < pc) if reverse else (pt + incl > pc)
    eye_side = (lax.broadcasted_iota(jnp.int32, (c, 2 * c), 0)
                == lax.broadcasted_iota(jnp.int32, (c, 2 * c), 1) % c).astype(F32)
    kk_scale = kk_ref[...]
    ka_scale = ka_ref[...]
    lvl = (lax.broadcasted_iota(jnp.int32, (c, 2 * c), 0)
           ^ (lax.broadcasted_iota(jnp.int32, (c, 2 * c), 1) % c))

    n_chunks = tt // c
    n_pairs = r_ref.shape[1] // lanes
    order = list(range(n_chunks - 1, -1, -1) if reverse else range(n_chunks))
    sls = [(slice(ci * c, (ci + 1) * c), slice(p * lanes, (p + 1) * lanes))
           for p in range(n_pairs) for ci in order]
    vs = [v_ref[sl] for sl in sls]
    pre = []
    for sl in sls:
        k = k_ref[sl]
        asig = a_ref[sl]
        ld = w_ref[sl]
        kk = k * kk_scale[:, sl[1]]
        kk2 = kk * kk
        sq = jnp.where(head0,
                       jnp.sum(jnp.where(head0, kk2, 0.0), axis=-1, keepdims=True),
                       jnp.sum(jnp.where(head0, 0.0, kk2), axis=-1, keepdims=True))
        kk = kk * lax.rsqrt(jnp.maximum(sq, 1e-24))
        bvec = kk * asig
        kd = k * (1.0 + (asig - 1.0) * ka_scale[:, sl[1]])
        cum = _dot_parts([tri_b], _split(ld, 3), 3)
        tot = cum[0:1, :] if reverse else cum[c - 1:c, :]
        e_neg = jnp.exp(-cum)
        e_rem = jnp.exp(tot - cum)
        pre.append(dict(at=-kk * jnp.exp(cum - ld), rt=r_ref[sl] * jnp.exp(cum),
                        bt=bvec * e_neg, kt=kd * e_neg, bc=bvec * e_rem, kc=kd * e_rem,
                        wc=jnp.exp(tot)))
    ps = []
    for q in pre:
        lhs = jnp.concatenate([q["at"], q["rt"]], axis=0)
        rhs = jnp.concatenate([_stack_heads(q["bt"], head0), _stack_heads(q["kt"], head0)], axis=0)
        ps.append(jnp.where(pmask, _dot_nt(lhs, rhs), 0.0))
    a_ab = [p[:c, :2 * c] for p in ps]
    a_ak = [p[:c, 2 * c:] for p in ps]
    p_rb = [p[c:, :2 * c] for p in ps]
    p_rk = [p[c:, 2 * c:] for p in ps]
    base = [jnp.where((lvl >> 3) == 0, a, 0.0) for a in a_ab]
    pw2 = [_heads_mm(x, x, head0, False) for x in base]
    pw4 = [_heads_mm(x, x, head0, False) for x in pw2]
    inv = [eye_side + x for x in base]
    inv = [d + _heads_mm(d, x, head0, False) for d, x in zip(inv, pw2)]
    inv = [d + _heads_mm(d, x, head0, False) for d, x in zip(inv, pw4)]
    for bit in (3, 4, 5):
        off = [jnp.where((lvl >> bit) == 1, a, 0.0) for a in a_ab]
        ed = [_heads_mm(e, d, head0, False) for e, d in zip(off, inv)]
        inv = [d + _heads_mm(d, x, head0, False) for d, x in zip(inv, ed)]
    res = [eye_side - d + _heads_mm(a, d, head0, True) for a, d in zip(a_ab, inv)]
    inv = [d + _heads_mm(d, x, head0, False) for d, x in zip(inv, res)]
    at2 = [_heads_mm(d, q["at"], head0, True) for d, q in zip(inv, pre)]
    av = [_heads_mm(a, v, head0, False) for a, v in zip(a_ak, vs)]
    u_loc = [_heads_mm(d, x, head0, True) for d, x in zip(inv, av)]
    r_eff = [q["rt"] + _heads_mm(p, x, head0, False) for q, p, x in zip(pre, p_rb, at2)]
    y_loc = [_heads_mm(pb, u, head0, False) + _heads_mm(pk, v, head0, False)
             for pb, u, pk, v in zip(p_rb, u_loc, p_rk, vs)]
    m_t = [eye2 * q["wc"] + jnp.where(same_head, _dot_tn(q["bc"], x), 0.0)
           for q, x in zip(pre, at2)]
    n_t = [jnp.where(same_head, _dot_tn(q["bc"], u) + _dot_tn(q["kc"], v), 0.0)
           for q, u, v in zip(pre, u_loc, vs)]
    for p in range(n_pairs):
        z = z_ref[p]
        for i in range(p * n_chunks, (p + 1) * n_chunks):
            o_ref[sls[i]] = _dot(r_eff[i], z) + y_loc[i]
            z = _dot2(m_t[i], z) + n_t[i]
        z_ref[p] = z


def _rwkv_scan(r, k, v, asig, logw, k_k, k_a, n_seq, reverse):
    b, n_tot, d = r.shape
    lanes = 2 * RWKV_N
    tt = _pick(n_tot - n_seq, (256, 128, 64))
    assert n_seq % tt == 0 and d % lanes == 0
    n_lat, n_all = n_seq // tt, n_tot // tt
    n_pairs = _pick(d // lanes, (4, 2, 1))
    wide = n_pairs * lanes
    tile = pl.BlockSpec((None, tt, wide),
                        lambda i, h, j: (i, _scan_tile(j, n_lat, n_all, reverse), h))
    vec = pl.BlockSpec((1, wide), lambda i, h, j: (0, h))
    return pl.pallas_call(
        functools.partial(_rwkv_scan_kernel, reverse=reverse),
        grid=(b, d // wide, n_all),
        in_specs=[tile] * 5 + [vec, vec],
        out_specs=tile,
        out_shape=jax.ShapeDtypeStruct((b, n_tot, d), F32),
        scratch_shapes=[pltpu.VMEM((n_pairs, lanes, lanes), F32)],
        compiler_params=_cparams("parallel", "parallel", "arbitrary"),
        name="rwkv_bwd" if reverse else "rwkv_fwd",
    )(r, k, v, asig, logw, k_k.reshape(1, d), k_a.reshape(1, d))


def _rwkv_read_kernel(y0_ref, y1_ref, r_ref, k_ref, v_ref, a0_ref, a1_ref, g_ref,
                      ka_ref, rk_ref, lw_ref, lb_ref, o_ref):
    lanes = 2 * RWKV_N
    r2 = lax.broadcasted_iota(jnp.int32, (lanes, lanes), 0)
    c2 = lax.broadcasted_iota(jnp.int32, (lanes, lanes), 1)
    ones_bd = ((r2 < RWKV_N) == (c2 < RWKV_N)).astype(BF16)
    head_sum = lambda t: _dot_parts(_split(t, 3), [ones_bd], 3)
    for n in range(r_ref.shape[1] // lanes):
        sl = slice(n * lanes, (n + 1) * lanes)
        y = y0_ref[:, sl] + y1_ref[:, sl]
        mean = head_sum(y) * (1.0 / RWKV_N)
        yc = y - mean
        var = head_sum(yc * yc) * (1.0 / RWKV_N)
        y = yc * lax.rsqrt(var + RWKV_GN_EPS) * lw_ref[:, sl] + lb_ref[:, sl]
        r = r_ref[:, sl]
        k = k_ref[:, sl]
        ka = ka_ref[:, sl]
        k0 = k * (1.0 + (a0_ref[:, sl] - 1.0) * ka)
        k1 = k * (1.0 + (a1_ref[:, sl] - 1.0) * ka)
        rk = rk_ref[:, sl]
        bonus = head_sum(r * k0 * rk + r * k1 * rk) * v_ref[:, sl]
        o_ref[:, sl] = ((y + bonus) * g_ref[:, sl]).astype(o_ref.dtype)


def _rwkv_read(y0, y1, r, k, v, a0, a1, g, k_a, r_k, ln_w, ln_b):
    m, d = r.shape
    tm = _pick(m, (128, 64, 32, 16, 8))
    tile = pl.BlockSpec((tm, d), lambda i: (i, 0))
    vec = pl.BlockSpec((1, d), lambda i: (0, 0))
    return pl.pallas_call(
        _rwkv_read_kernel,
        grid=(m // tm,),
        in_specs=[tile] * 8 + [vec] * 4,
        out_specs=tile,
        out_shape=jax.ShapeDtypeStruct((m, d), BF16),
        compiler_params=_cparams("parallel"),
        name="rwkv_readout",
    )(y0, y1, r, k, v, a0, a1, g, k_a.reshape(1, d), r_k.reshape(1, d),
      ln_w.reshape(1, d), ln_b.reshape(1, d))


def _pad_rows(w2, dr):
    zeros = jnp.zeros_like(w2[0])
    return jnp.concatenate([w2[0], zeros] if dr == 0 else [zeros, w2[1]], axis=0)


def _rwkv_mixer(us, n_seq, j, mu, w_rkv, w0, w1, w2, a0, a1, a2, g1, g2, k_k, k_a, r_k,
                ln_w, ln_b, w_out):
    b, n_tot, d = us.shape
    m = b * n_tot
    xr, xw, xk, xv, xa, xg = [t.reshape(m, d) for t in _rwkv_mix(us, mu, n_seq)]
    r = _mm(xr, w_rkv, widx=(j, 0))
    k = _mm(xk, w_rkv, widx=(j, 1))
    v = _mm(xv, w_rkv, widx=(j, 2))
    w_low = _mm(xw, jnp.concatenate([w1[0], w1[1]], axis=-1), act="tanh")
    a_low = _mm(xa, jnp.concatenate([a1[0], a1[1]], axis=-1))
    g_low = _mm(xg, g1, act="sigmoid")
    gate = _mm(g_low, g2)
    sh = (b, n_tot, d)
    ys, asigs = [], []
    for dr in range(2):
        logw = _mm(w_low, _pad_rows(w2, dr), w0[dr], act="rwkv_decay")
        asig = _mm(a_low, _pad_rows(a2, dr), a0[dr], act="sigmoid")
        asigs.append(asig)
        ys.append(_rwkv_scan(r.reshape(sh), k.reshape(sh), v.reshape(sh), asig.reshape(sh),
                             logw.reshape(sh), k_k, k_a, n_seq, dr == 1).reshape(m, d))
    hy = _rwkv_read(ys[0], ys[1], r, k, v, asigs[0], asigs[1], gate, k_a, r_k.reshape(-1),
                    ln_w, ln_b)
    return _mm(hy, w_out, widx=(j,)).reshape(b, n_tot, d)


def kernel(x, c, ctx, c_ctx, ada_w, ada_b, norm_g, ffn_w_in, ffn_w_out, lru_w_in, lru_b_in, lru_conv_w, lru_conv_b, lru_gate_w, lru_gate_b, lru_log_lambda, lru_w_out, lru_b_out, gla_w_in, gla_b_r, gla_gate_w1, gla_gate_w2, gla_gate_b, gla_norm_g, gla_w_out, rwkv_mu, rwkv_w_rkv, rwkv_w0, rwkv_w1, rwkv_w2, rwkv_a0, rwkv_a1, rwkv_a2, rwkv_g1, rwkv_g2, rwkv_k_k, rwkv_k_a, rwkv_r_k, rwkv_ln_w, rwkv_ln_b, rwkv_w_out):
    b, n_seq, d = x.shape
    depth = ada_w.shape[0]
    assert b < 8
    cond = jnp.concatenate([c, c_ctx[None, :], jnp.zeros((8 - b - 1, d), F32)], axis=0)
    mods = _mods(cond, ada_w, ada_b).reshape(depth, 8, N_MOD, d)
    h_lat, h_ctx = x, ctx
    for i in range(depth):
        ctx_out = i < depth - 1
        kind, j = i % N_MIXERS, i // N_MIXERS
        col_major = i % 2 == 1
        m_lat, m_ctx = mods[i, :b], mods[i, b]
        us = _norm_mod(h_lat, h_ctx, norm_g[i, 0], m_lat, m_ctx, 0, 1, col_major,
                       F32 if kind == 2 else BF16)
        if kind == 0:
            y = _rglru_mixer(us, n_seq, j, lru_w_in, lru_b_in[j], lru_conv_w[j], lru_conv_b[j],
                             lru_gate_w[j], lru_gate_b[j], lru_log_lambda[j], lru_w_out,
                             lru_b_out[j])
        elif kind == 1:
            y = _gla_mixer(us, n_seq, j, gla_w_in, gla_b_r[j], gla_gate_w1[j], gla_gate_w2[j],
                           gla_gate_b[j], gla_norm_g[j], gla_w_out)
        else:
            y = _rwkv_mixer(us, n_seq, j, rwkv_mu[j], rwkv_w_rkv, rwkv_w0[j], rwkv_w1[j],
                            rwkv_w2[j], rwkv_a0[j], rwkv_a1[j], rwkv_a2[j], rwkv_g1[j],
                            rwkv_g2[j], rwkv_k_k[j], rwkv_k_a[j], rwkv_r_k[j], rwkv_ln_w[j],
                            rwkv_ln_b[j], rwkv_w_out)
        h_lat, h_ctx = _resid(h_lat, h_ctx if ctx_out else None, y, norm_g[i, 1], m_lat, m_ctx,
                              2, col_major)
        us = _norm_mod(h_lat, h_ctx, norm_g[i, 2], m_lat, m_ctx, 3, 4, False, BF16)
        n_tot = us.shape[1]
        mid = _ffn_in(us.reshape(b * n_tot, d), ffn_w_in, i)
        y = _mm(mid, ffn_w_out, widx=(i,)).reshape(b, n_tot, d)
        h_lat, h_ctx = _resid(h_lat, h_ctx, y, norm_g[i, 3], m_lat, m_ctx, 5, False)
    return h_lat
```

```python
import functools

import jax
import jax.numpy as jnp
from jax import lax
from jax.experimental import pallas as pl
from jax.experimental.pallas import tpu as pltpu

F32 = jnp.float32
BF16 = jnp.bfloat16

GRID_W = 64
N_MIXERS = 3
N_MOD = 6
NORM_EPS = 1e-6
LRU_BLOCKS = 16
LRU_C = 8.0
GLA_HEADS = 4
GLA_TAU = 16.0
GLA_CHUNK = 64
RWKV_N = 64
RWKV_CHUNK = 64
RWKV_GN_EPS = 64e-5
LANES = 128
SUBLANES = 8
VMEM_LIMIT = 52 * 1024 * 1024
PROJ_VMEM_BUDGET = 44 * 1024 * 1024


def _cparams(*sem):
    return pltpu.CompilerParams(dimension_semantics=sem, vmem_limit_bytes=VMEM_LIMIT)


def _pick(n, cands):
    for c in cands:
        if n % c == 0:
            return c
    return n


def _dot(a, b):
    return jnp.dot(a.astype(BF16), b.astype(BF16), preferred_element_type=F32)


def _dot_nt(a, b):
    return lax.dot_general(a.astype(BF16), b.astype(BF16), (((1,), (1,)), ((), ())),
                           preferred_element_type=F32)


def _dot_tn(a, b):
    return lax.dot_general(a.astype(BF16), b.astype(BF16), (((0,), (0,)), ((), ())),
                           preferred_element_type=F32)


def _sigmoid(x):
    return 1.0 / (1.0 + jnp.exp(-x))


def _softplus(x):
    return jnp.maximum(x, 0.0) + jnp.log(1.0 + jnp.exp(-jnp.abs(x)))


def _mods_kernel(x_ref, w_ref, b_ref, o_ref):
    x = x_ref[...]
    x = x * _sigmoid(x)
    o_ref[...] = _dot(x, w_ref[...]) + b_ref[...]


def _mods(cond, ada_w, ada_b):
    n_layers, d, n = ada_w.shape
    tn = _pick(n, (1024, 512, 256, 128))
    return pl.pallas_call(
        _mods_kernel,
        grid=(n_layers, n // tn),
        in_specs=[pl.BlockSpec((8, d), lambda l, j: (0, 0)),
                  pl.BlockSpec((None, d, tn), lambda l, j: (l, 0, j)),
                  pl.BlockSpec((None, 1, tn), lambda l, j: (l, 0, j))],
        out_specs=pl.BlockSpec((None, 8, tn), lambda l, j: (l, 0, j)),
        out_shape=jax.ShapeDtypeStruct((n_layers, 8, n), F32),
        compiler_params=_cparams("parallel", "parallel"),
        name="ada_mods",
    )(cond, ada_w, ada_b.reshape(n_layers, 1, n))


def _norm_mod_kernel(*refs, k_shift, k_scale, n_lat):
    def emit(h_ref, m_ref, o_ref):
        x = h_ref[...]
        y = x * lax.rsqrt(jnp.mean(x * x, axis=-1, keepdims=True) + NORM_EPS) * g_ref[...]
        y = y * (1.0 + m_ref[k_scale:k_scale + 1, :]) + m_ref[k_shift:k_shift + 1, :]
        o_ref[...] = y.astype(o_ref.dtype)

    if n_lat is None:
        h_ref, g_ref, m_ref, o_ref = refs
        emit(h_ref, m_ref, o_ref)
        return
    h_ref, c_ref, g_ref, m_ref, mc_ref, o_ref = refs
    is_ctx = pl.program_id(1) >= n_lat

    @pl.when(jnp.logical_not(is_ctx))
    def _():
        emit(h_ref, m_ref, o_ref)

    @pl.when(is_ctx)
    def _():
        emit(c_ref, mc_ref, o_ref)


def _norm_mod(h_lat, h_ctx, g, m_lat, m_ctx, k_shift, k_scale, col_major, out_dtype):
    b, n, d = h_lat.shape
    rows = n // GRID_W
    n_ctx = 0 if h_ctx is None else h_ctx.shape[1]
    g = g.reshape(1, d)
    if col_major:
        tt = rows
        n_lat = GRID_W
        h_in = h_lat.reshape(b, rows, GRID_W * d)
        h_spec = pl.BlockSpec((None, rows, d), lambda i, j: (i, 0, jnp.minimum(j, n_lat - 1)))
    else:
        tt = _pick(n_ctx if n_ctx else n, (256, 128, 64, 32, 16, 8))
        n_lat = n // tt
        h_in = h_lat
        h_spec = pl.BlockSpec((None, tt, d), lambda i, j: (i, jnp.minimum(j, n_lat - 1), 0))
    assert n_ctx % tt == 0 and n % tt == 0
    g_spec = pl.BlockSpec((1, d), lambda i, j: (0, 0))
    m_spec = pl.BlockSpec((None, N_MOD, d), lambda i, j: (i, 0, 0))
    if h_ctx is None:
        in_specs, args = [h_spec, g_spec, m_spec], (h_in, g, m_lat)
    else:
        in_specs = [h_spec,
                    pl.BlockSpec((None, tt, d), lambda i, j: (i, jnp.maximum(j - n_lat, 0), 0)),
                    g_spec, m_spec,
                    pl.BlockSpec((None, N_MOD, d), lambda i, j: (0, 0, 0))]
        args = (h_in, h_ctx, g, m_lat, m_ctx.reshape(1, N_MOD, d))
    return pl.pallas_call(
        functools.partial(_norm_mod_kernel, k_shift=k_shift, k_scale=k_scale,
                          n_lat=None if h_ctx is None else n_lat),
        grid=(b, n_lat + n_ctx // tt),
        in_specs=in_specs,
        out_specs=pl.BlockSpec((None, tt, d), lambda i, j: (i, j, 0)),
        out_shape=jax.ShapeDtypeStruct((b, n + n_ctx, d), out_dtype),
        compiler_params=_cparams("parallel", "arbitrary"),
        name="norm_mod",
    )(*args)


def _resid_kernel(*refs, k_gate, follow, n_lat):
    it = iter(refs)
    both = n_lat is not None
    h_ref = next(it)
    c_ref = next(it) if both else None
    y_ref, g_ref = next(it), next(it)
    g2_ref = next(it) if follow else None
    m_ref = next(it)
    mc_ref = next(it) if both else None
    oh_ref = next(it)
    oc_ref = next(it) if both else None
    ou_ref = next(it) if follow else None

    def emit(src, mods, dst):
        y = y_ref[...]
        yn = y * lax.rsqrt(jnp.mean(y * y, axis=-1, keepdims=True) + NORM_EPS) * g_ref[...]
        hn = src[...] + mods[k_gate:k_gate + 1, :] * yn
        dst[...] = hn
        if follow:
            k_shift, k_scale = follow
            u = hn * lax.rsqrt(jnp.mean(hn * hn, axis=-1, keepdims=True) + NORM_EPS) * g2_ref[...]
            u = u * (1.0 + mods[k_scale:k_scale + 1, :]) + mods[k_shift:k_shift + 1, :]
            ou_ref[...] = u.astype(ou_ref.dtype)

    if not both:
        emit(h_ref, m_ref, oh_ref)
        return
    is_ctx = pl.program_id(1) >= n_lat

    @pl.when(jnp.logical_not(is_ctx))
    def _():
        emit(h_ref, m_ref, oh_ref)

    @pl.when(is_ctx)
    def _():
        emit(c_ref, mc_ref, oc_ref)


def _resid(h_lat, h_ctx, y, g, m_lat, m_ctx, k_gate, col_major, follow=None, g2=None):
    b, n, d = h_lat.shape
    rows = n // GRID_W
    n_ctx = 0 if h_ctx is None else h_ctx.shape[1]
    both = h_ctx is not None
    if col_major:
        tt = rows
        n_lat = GRID_W
        h_in = h_lat.reshape(b, rows, GRID_W * d)
        h_spec = pl.BlockSpec((None, rows, d), lambda i, j: (i, 0, jnp.minimum(j, n_lat - 1)))
        lat_shape = jax.ShapeDtypeStruct((b, rows, GRID_W * d), F32)
    else:
        tt = _pick(n_ctx if n_ctx else n, (256, 128, 64, 32, 16, 8))
        n_lat = n // tt
        h_in = h_lat
        h_spec = pl.BlockSpec((None, tt, d), lambda i, j: (i, jnp.minimum(j, n_lat - 1), 0))
        lat_shape = jax.ShapeDtypeStruct((b, n, d), F32)
    assert n_ctx % tt == 0 and n % tt == 0
    c_spec = pl.BlockSpec((None, tt, d), lambda i, j: (i, jnp.maximum(j - n_lat, 0), 0))
    y_spec = pl.BlockSpec((None, tt, d), lambda i, j: (i, j, 0))
    g_spec = pl.BlockSpec((1, d), lambda i, j: (0, 0))
    in_specs, args = [h_spec], [h_in]
    if both:
        in_specs.append(c_spec)
        args.append(h_ctx)
    in_specs += [y_spec, g_spec]
    args += [y, g.reshape(1, d)]
    if follow:
        in_specs.append(g_spec)
        args.append(g2.reshape(1, d))
    in_specs.append(pl.BlockSpec((None, N_MOD, d), lambda i, j: (i, 0, 0)))
    args.append(m_lat)
    out_specs, out_shape = [h_spec], [lat_shape]
    if both:
        in_specs.append(pl.BlockSpec((None, N_MOD, d), lambda i, j: (0, 0, 0)))
        args.append(m_ctx.reshape(1, N_MOD, d))
        out_specs.append(c_spec)
        out_shape.append(jax.ShapeDtypeStruct((b, n_ctx, d), F32))
    if follow:
        out_specs.append(y_spec)
        out_shape.append(jax.ShapeDtypeStruct((b, n + n_ctx, d), BF16))
    outs = pl.pallas_call(
        functools.partial(_resid_kernel, k_gate=k_gate, follow=follow,
                          n_lat=n_lat if both else None),
        grid=(b, n_lat + n_ctx // tt),
        in_specs=in_specs,
        out_specs=out_specs,
        out_shape=out_shape,
        compiler_params=_cparams("parallel", "arbitrary"),
        name="resid",
    )(*args)
    outs = list(outs)
    new_lat = outs.pop(0).reshape(b, n, d)
    new_ctx = outs.pop(0) if both else None
    return new_lat, new_ctx, (outs.pop(0) if follow else None)


def _gelu_tanh(x):
    return 0.5 * x * (1.0 + jnp.tanh(0.7978845608028654 * (x + 0.044715 * x * x * x)))


def _apply_act(z, act):
    if act is None:
        return z
    if act == "gelu":
        return _gelu_tanh(z)
    if act == "tanh":
        return jnp.tanh(z)
    if act == "sigmoid":
        return _sigmoid(z)
    if act == "gla_gate":
        return -_softplus(-z) * (1.0 / GLA_TAU)
    if act == "rwkv_decay":
        return -jnp.exp(-_softplus(-z) - 0.5)
    raise ValueError(act)


def _proj_tiles(m, k, n, x_bytes, o_bytes, n_weights):
    best = None
    for tm in (2048, 1536, 1024, 512, 256, 128, 64, 32, 16, 8):
        if m % tm:
            continue
        for tn in (1024, 512, 256, 128, n):
            if n % tn:
                continue
            need = (2 * tm * k * x_bytes + n_weights * k * tn * (2 * 4 + 2)
                    + 2 * tm * tn * o_bytes + (tm * tn * 4 if o_bytes < 4 else 0))
            if need <= PROJ_VMEM_BUDGET and (best is None or tm * tn > best[0] * best[1]):
                best = (tm, tn)
    assert best is not None
    return best


def _row_chunks(tm, rows=512):
    rows = min(rows, tm)
    return [slice(r, r + rows) for r in range(0, tm, rows)]


def _mm_kernel(x_ref, w_ref, b_ref, o_ref, wb, *, act):
    @pl.when(pl.program_id(1) == 0)
    def _():
        wb[...] = w_ref[...].astype(BF16)

    for rows in _row_chunks(x_ref.shape[0]):
        z = _dot(x_ref[rows, :], wb[...]) + b_ref[...]
        o_ref[rows, :] = _apply_act(z, act).astype(o_ref.dtype)


def _mm(x, w, bias=None, act=None, out_dtype=F32, col0=0, ncols=None, widx=()):
    m, k = x.shape
    n_total = w.shape[-1]
    ncols = n_total if ncols is None else ncols
    tm, tn = _proj_tiles(m, k, ncols, x.dtype.itemsize, jnp.dtype(out_dtype).itemsize, 1)
    assert col0 % tn == 0 and w.ndim == 2 + len(widx)
    off = col0 // tn
    if bias is None:
        bias = jnp.zeros((ncols,), F32)
    return pl.pallas_call(
        functools.partial(_mm_kernel, act=act),
        grid=(ncols // tn, m // tm),
        in_specs=[pl.BlockSpec((tm, k), lambda j, i: (i, 0)),
                  pl.BlockSpec((None,) * len(widx) + (k, tn), lambda j, i: widx + (0, off + j)),
                  pl.BlockSpec((1, tn), lambda j, i: (0, j))],
        out_specs=pl.BlockSpec((tm, tn), lambda j, i: (i, j)),
        out_shape=jax.ShapeDtypeStruct((m, ncols), out_dtype),
        scratch_shapes=[pltpu.VMEM((k, tn), BF16)],
        compiler_params=_cparams("parallel", "arbitrary"),
        name="proj",
    )(x, w, bias.reshape(1, ncols).astype(F32))


def _ffn_in_kernel(x_ref, wg_ref, wu_ref, o_ref, wgb, wub):
    @pl.when(pl.program_id(1) == 0)
    def _():
        wgb[...] = wg_ref[...].astype(BF16)
        wub[...] = wu_ref[...].astype(BF16)

    for rows in _row_chunks(x_ref.shape[0]):
        x = x_ref[rows, :]
        gate = _dot(x, wgb[...])
        up = _dot(x, wub[...])
        o_ref[rows, :] = (gate * _sigmoid(gate) * up).astype(o_ref.dtype)


def _ffn_in(x, w_in, layer):
    m, k = x.shape
    d_ff = w_in.shape[-1] // 2
    tm, tn = _proj_tiles(m, k, d_ff, x.dtype.itemsize, 2, 2)
    off = d_ff // tn
    return pl.pallas_call(
        _ffn_in_kernel,
        grid=(d_ff // tn, m // tm),
        in_specs=[pl.BlockSpec((tm, k), lambda j, i: (i, 0)),
                  pl.BlockSpec((None, k, tn), lambda j, i: (layer, 0, j)),
                  pl.BlockSpec((None, k, tn), lambda j, i: (layer, 0, off + j))],
        out_specs=pl.BlockSpec((tm, tn), lambda j, i: (i, j)),
        out_shape=jax.ShapeDtypeStruct((m, d_ff), BF16),
        scratch_shapes=[pltpu.VMEM((k, tn), BF16)] * 2,
        compiler_params=_cparams("parallel", "arbitrary"),
        name="ffn_in",
    )(x, w_in, w_in)


def _scan_tile(step, n_lat, n_all, reverse):
    if reverse:
        return n_all - 1 - step
    n_ctx = n_all - n_lat
    return jnp.where(step < n_ctx, n_lat + step, step - n_ctx)


def _lru_conv_kernel(x_ref, p_ref, n_ref, w_ref, b_ref, o_ref, buf, *, n_lat, n_all):
    t = pl.program_id(1)
    tt = x_ref.shape[0]
    first = jnp.logical_or(t == 0, t == n_lat)
    last = jnp.logical_or(t == n_lat - 1, t == n_all - 1)
    buf[0:SUBLANES, :] = jnp.where(first, 0.0, p_ref[...])
    buf[SUBLANES:SUBLANES + tt, :] = x_ref[...]
    buf[SUBLANES + tt:2 * SUBLANES + tt, :] = jnp.where(last, 0.0, n_ref[...])
    acc = b_ref[...] + w_ref[0:1, :] * buf[SUBLANES - 2:SUBLANES - 2 + tt, :]
    acc = acc + w_ref[1:2, :] * buf[SUBLANES - 1:SUBLANES - 1 + tt, :]
    acc = acc + w_ref[2:3, :] * buf[SUBLANES:SUBLANES + tt, :]
    acc = acc + w_ref[3:4, :] * buf[SUBLANES + 1:SUBLANES + 1 + tt, :]
    o_ref[...] = acc


def _lru_conv(rec, conv_w, conv_b, n_seq):
    b, n_tot, w = rec.shape
    tt = _pick(n_tot - n_seq, (256, 128, 64, 32, 16, 8))
    assert n_seq % tt == 0 and conv_w.shape[0] == 4
    n_lat, n_all = n_seq // tt, n_tot // tt
    r8 = tt // SUBLANES
    last8 = n_tot // SUBLANES - 1
    return pl.pallas_call(
        functools.partial(_lru_conv_kernel, n_lat=n_lat, n_all=n_all),
        grid=(b, n_all),
        in_specs=[pl.BlockSpec((None, tt, w), lambda i, j: (i, j, 0)),
                  pl.BlockSpec((None, SUBLANES, w), lambda i, j: (i, jnp.maximum(j * r8 - 1, 0), 0)),
                  pl.BlockSpec((None, SUBLANES, w), lambda i, j: (i, jnp.minimum((j + 1) * r8, last8), 0)),
                  pl.BlockSpec((4, w), lambda i, j: (0, 0)),
                  pl.BlockSpec((1, w), lambda i, j: (0, 0))],
        out_specs=pl.BlockSpec((None, tt, w), lambda i, j: (i, j, 0)),
        out_shape=jax.ShapeDtypeStruct((b, n_tot, w), F32),
        scratch_shapes=[pltpu.VMEM((tt + 2 * SUBLANES, w), F32)],
        compiler_params=_cparams("parallel", "parallel"),
        name="lru_conv",
    )(rec, rec, rec, conv_w, conv_b.reshape(1, w))


def _lru_scan_kernel(*refs, reverse, combine, blk):
    if combine:
        x_ref, wc_ref, gb_ref, ll_ref, hf_ref, gate_ref, o_ref, a_s, b_s, carry = refs
    else:
        x_ref, wc_ref, gb_ref, ll_ref, o_ref, a_s, b_s, carry = refs
    tt, w = x_ref.shape
    nlb = w // LANES

    @pl.when(pl.program_id(1) == 0)
    def _():
        carry[...] = jnp.zeros_like(carry)

    def put(dst, n, val):
        if blk >= LANES:
            for p in range(blk // LANES):
                dst[n * (blk // LANES) + p] = val[:, p * LANES:(p + 1) * LANES]
        else:
            off = (n * blk) % LANES
            dst[(n * blk) // LANES, :, off:off + blk] = val

    sp = _softplus(-ll_ref[...])
    for n in range(w // blk):
        sl = slice(n * blk, (n + 1) * blk)
        xb = x_ref[:, sl]
        gates = _dot(xb, wc_ref[n])
        rr = _sigmoid(gates[:, :blk] + gb_ref[0:1, sl])
        ii = _sigmoid(gates[:, blk:] + gb_ref[1:2, sl])
        a = jnp.exp(-LRU_C * rr * sp[:, sl])
        put(a_s, n, a)
        put(b_s, n, jnp.sqrt(1.0 - a * a) * (ii * xb))

    groups = tt // SUBLANES
    row = lax.broadcasted_iota(jnp.int32, (groups, SUBLANES, LANES), 1)
    for n in range(nlb):
        a = a_s[n].reshape(groups, SUBLANES, LANES)
        bt = b_s[n].reshape(groups, SUBLANES, LANES)
        for s in (1, 2, 4):
            keep = (row < SUBLANES - s) if reverse else (row >= s)
            shift = SUBLANES - s if reverse else s
            a_sh = jnp.where(keep, pltpu.roll(a, shift, 1), 1.0)
            b_sh = jnp.where(keep, pltpu.roll(bt, shift, 1), 0.0)
            bt = a * b_sh + bt
            a = a * a_sh
        c = carry[n]
        outs = [None] * groups
        for g in (range(groups - 1, -1, -1) if reverse else range(groups)):
            h = a[g] * c + bt[g]
            outs[g] = h
            c = h[0:1, :] if reverse else h[SUBLANES - 1:SUBLANES, :]
        carry[n] = c
        hl = jnp.concatenate(outs, axis=0)
        sl = slice(n * LANES, (n + 1) * LANES)
        if combine:
            o_ref[:, sl] = ((hl + hf_ref[:, sl]) * gate_ref[:, sl]).astype(o_ref.dtype)
        else:
            o_ref[:, sl] = hl


def _lru_scan(x, wcat, gate_b, log_lambda, n_seq, reverse, h_fwd=None, gate=None):
    b, n_tot, w = x.shape
    blk = w // LRU_BLOCKS
    tt = _pick(n_tot - n_seq, (256, 128, 64))
    assert n_seq % tt == 0 and tt % (SUBLANES * SUBLANES) == 0
    n_lat, n_all = n_seq // tt, n_tot // tt
    combine = h_fwd is not None
    tile = lambda i, j: (i, _scan_tile(j, n_lat, n_all, reverse), 0)
    in_specs = [pl.BlockSpec((None, tt, w), tile),
                pl.BlockSpec((LRU_BLOCKS, blk, 2 * blk), lambda i, j: (0, 0, 0)),
                pl.BlockSpec((2, w), lambda i, j: (0, 0)),
                pl.BlockSpec((1, w), lambda i, j: (0, 0))]
    args = [x, wcat, gate_b, log_lambda.reshape(1, w)]
    if combine:
        in_specs += [pl.BlockSpec((None, tt, w), tile), pl.BlockSpec((None, tt, w), tile)]
        args += [h_fwd, gate]
    return pl.pallas_call(
        functools.partial(_lru_scan_kernel, reverse=reverse, combine=combine, blk=blk),
        grid=(b, n_all),
        in_specs=in_specs,
        out_specs=pl.BlockSpec((None, tt, w), tile),
        out_shape=jax.ShapeDtypeStruct((b, n_tot, w), BF16 if combine else F32),
        scratch_shapes=([pltpu.VMEM((w // LANES, tt, LANES), F32)] * 2
                        + [pltpu.VMEM((w // LANES, 1, LANES), F32)]),
        compiler_params=_cparams("parallel", "arbitrary"),
        name="lru_scan_bwd" if reverse else "lru_scan_fwd",
    )(*args)


def _rglru_mixer(us, n_seq, j, w_in, b_in, conv_w, conv_b, gate_w, gate_b, log_lambda, w_out, b_out):
    b, n_tot, d = us.shape
    w = w_in.shape[-1] // 2
    x2 = us.reshape(b * n_tot, d)
    gate = _mm(x2, w_in, b_in[:w], act="gelu", col0=0, ncols=w, widx=(j,)).reshape(b, n_tot, w)
    rec = _mm(x2, w_in, b_in[w:], col0=w, ncols=w, widx=(j,)).reshape(b, n_tot, w)
    x = _lru_conv(rec, conv_w, conv_b, n_seq)
    wcat = jnp.concatenate([gate_w[:, 0], gate_w[:, 1]], axis=-1).astype(BF16)
    h_f = _lru_scan(x, wcat[0], gate_b[0], log_lambda[0], n_seq, False)
    hy = _lru_scan(x, wcat[1], gate_b[1], log_lambda[1], n_seq, True, h_fwd=h_f, gate=gate)
    y = _mm(hy.reshape(b * n_tot, w), w_out, b_out, widx=(j,))
    return y.reshape(b, n_tot, d)


def _gla_kernel(*refs, reverse, combine, scale):
    if combine:
        q_ref, k_ref, v_ref, g_ref, of_ref, o_ref, st = refs
    else:
        q_ref, k_ref, v_ref, g_ref, o_ref, st = refs
    tt = q_ref.shape[0]
    c = GLA_CHUNK

    @pl.when(pl.program_id(2) == 0)
    def _():
        st[...] = jnp.zeros_like(st)

    row = lax.broadcasted_iota(jnp.int32, (c, c), 0)
    col = lax.broadcasted_iota(jnp.int32, (c, c), 1)
    tri = (row <= col) if reverse else (row >= col)
    tri_b = tri.astype(BF16)
    n_chunks = tt // c
    n_heads, hv, hk = st.shape
    work = []
    for p in range(n_heads):
        lk = slice(p * hk, (p + 1) * hk)
        lv = slice(p * hv, (p + 1) * hv)
        for ci in (range(n_chunks - 1, -1, -1) if reverse else range(n_chunks)):
            sl = slice(ci * c, (ci + 1) * c)
            k = k_ref[sl, lk]
            v = v_ref[sl, lv].astype(BF16)
            bcum = _dot_parts([tri_b], _split(g_ref[sl, lk], 3), 3)
            b_last = bcum[0:1, :] if reverse else bcum[c - 1:c, :]
            q_e = (q_ref[sl, lk] * scale * jnp.exp(bcum)).astype(BF16)
            k_e = k * jnp.exp(-bcum)
            att = jnp.where(tri, _dot_nt(q_e, k_e), 0.0)
            o = _dot(att, v)
            if combine:
                o = o + of_ref[sl, lv]
            kv = _dot_tn(v, k * jnp.exp(b_last - bcum))
            work.append((sl, lv, q_e, o, jnp.exp(b_last), kv))
    for p in range(n_heads):
        s_t = st[p]
        for sl, lv, q_e, o, decay, kv in work[p * n_chunks:(p + 1) * n_chunks]:
            o_ref[sl, lv] = o + _dot_nt(q_e, s_t)
            s_t = s_t * decay + kv
        st[p] = s_t


def _gla_scan(qkvr, g, n_seq, reverse, o_fwd=None):
    b, n_tot, _ = qkvr.shape
    dk = g.shape[-1]
    hk = dk // GLA_HEADS
    hv = 2 * hk
    tt = _pick(n_tot - n_seq, (256, 128, 64))
    assert n_seq % tt == 0
    n_lat, n_all = n_seq // tt, n_tot // tt
    combine = o_fwd is not None
    hps = 2
    wk, wv = hps * hk, hps * hv
    kq, kv = dk // wk, 2 * dk // wv
    tile = lambda i, h, j: _scan_tile(j, n_lat, n_all, reverse)
    in_specs = [pl.BlockSpec((None, tt, wk), lambda i, h, j: (i, tile(i, h, j), h)),
                pl.BlockSpec((None, tt, wk), lambda i, h, j: (i, tile(i, h, j), kq + h)),
                pl.BlockSpec((None, tt, wv), lambda i, h, j: (i, tile(i, h, j), kv + h)),
                pl.BlockSpec((None, tt, wk), lambda i, h, j: (i, tile(i, h, j), h))]
    args = [qkvr, qkvr, qkvr, g]
    o_spec = pl.BlockSpec((None, tt, wv), lambda i, h, j: (i, tile(i, h, j), h))
    if combine:
        in_specs.append(o_spec)
        args.append(o_fwd)
    return pl.pallas_call(
        functools.partial(_gla_kernel, reverse=reverse, combine=combine, scale=float(hk) ** -0.5),
        grid=(b, GLA_HEADS // hps, n_all),
        in_specs=in_specs,
        out_specs=o_spec,
        out_shape=jax.ShapeDtypeStruct((b, n_tot, 2 * dk), F32),
        scratch_shapes=[pltpu.VMEM((hps, hv, hk), F32)],
        compiler_params=_cparams("parallel", "parallel", "arbitrary"),
        name="gla_bwd" if reverse else "gla_fwd",
    )(*args)


def _gla_read_kernel(o_ref, r_ref, br_ref, g_ref, y_ref, *, hv):
    for h in range(GLA_HEADS):
        sl = slice(h * hv, (h + 1) * hv)
        o = o_ref[:, sl]
        o = o * lax.rsqrt(jnp.mean(o * o, axis=-1, keepdims=True) + NORM_EPS) * g_ref[:, sl]
        r = r_ref[:, sl] + br_ref[:, sl]
        y_ref[:, sl] = (o * (r * _sigmoid(r))).astype(y_ref.dtype)


def _gla_read(o, qkvr, b_r, norm_g):
    m, dv = o.shape
    tm = _pick(m, (256, 128, 64, 32, 16, 8))
    r_off = qkvr.shape[1] // dv - 1
    return pl.pallas_call(
        functools.partial(_gla_read_kernel, hv=dv // GLA_HEADS),
        grid=(m // tm,),
        in_specs=[pl.BlockSpec((tm, dv), lambda i: (i, 0)),
                  pl.BlockSpec((tm, dv), lambda i: (i, r_off)),
                  pl.BlockSpec((1, dv), lambda i: (0, 0)),
                  pl.BlockSpec((1, dv), lambda i: (0, 0))],
        out_specs=pl.BlockSpec((tm, dv), lambda i: (i, 0)),
        out_shape=jax.ShapeDtypeStruct((m, dv), BF16),
        compiler_params=_cparams("parallel"),
        name="gla_readout",
    )(o, qkvr, b_r.reshape(1, dv), norm_g.reshape(1, dv))


def _gla_mixer(us, n_seq, j, w_in, b_r, gate_w1, gate_w2, gate_b, norm_g, w_out):
    b, n_tot, d = us.shape
    m = b * n_tot
    x2 = us.reshape(m, d)
    dk = gate_w2.shape[-1]
    qkvr = _mm(x2, w_in, widx=(j,))
    low = _mm(x2, jnp.concatenate([gate_w1[0], gate_w1[1]], axis=-1))
    zeros = jnp.zeros_like(gate_w2[0])
    o = None
    for dr in range(2):
        w2 = jnp.concatenate([gate_w2[0], zeros] if dr == 0 else [zeros, gate_w2[1]], axis=0)
        g = _mm(low, w2, gate_b[dr], act="gla_gate")
        o = _gla_scan(qkvr.reshape(b, n_tot, -1), g.reshape(b, n_tot, dk), n_seq, dr == 1, o_fwd=o)
    hy = _gla_read(o.reshape(m, 2 * dk), qkvr, b_r, norm_g)
    return _mm(hy, w_out, widx=(j,)).reshape(b, n_tot, d)


def _rwkv_mix_kernel(x_ref, p_ref, n_ref, mu_ref, *o_refs, n_lat, n_all):
    t = pl.program_id(1)
    tt = x_ref.shape[0]
    first = jnp.logical_or(t == 0, t == n_lat)
    last = jnp.logical_or(t == n_lat - 1, t == n_all - 1)
    x = x_ref[...]
    prev_row = jnp.where(first, 0.0, p_ref[SUBLANES - 1:SUBLANES, :])
    next_row = jnp.where(last, 0.0, n_ref[0:1, :])
    rid = lax.broadcasted_iota(jnp.int32, x.shape, 0)
    up = jnp.where(rid == 0, prev_row, pltpu.roll(x, 1, 0))
    dn = jnp.where(rid == tt - 1, next_row, pltpu.roll(x, tt - 1, 0))
    dx = 0.5 * (up + dn) - x
    for i, o_ref in enumerate(o_refs):
        o_ref[...] = (x + dx * mu_ref[i:i + 1, :]).astype(o_ref.dtype)


def _rwkv_mix(us, mu, n_seq):
    b, n_tot, d = us.shape
    n_mix = mu.shape[0]
    tt = _pick(n_tot - n_seq, (256, 128, 64, 32, 16, 8))
    assert n_seq % tt == 0
    n_lat, n_all = n_seq // tt, n_tot // tt
    r8 = tt // SUBLANES
    last8 = n_tot // SUBLANES - 1
    out = jax.ShapeDtypeStruct((b, n_tot, d), BF16)
    return pl.pallas_call(
        functools.partial(_rwkv_mix_kernel, n_lat=n_lat, n_all=n_all),
        grid=(b, n_all),
        in_specs=[pl.BlockSpec((None, tt, d), lambda i, j: (i, j, 0)),
                  pl.BlockSpec((None, SUBLANES, d), lambda i, j: (i, jnp.maximum(j * r8 - 1, 0), 0)),
                  pl.BlockSpec((None, SUBLANES, d), lambda i, j: (i, jnp.minimum((j + 1) * r8, last8), 0)),
                  pl.BlockSpec((n_mix, d), lambda i, j: (0, 0))],
        out_specs=[pl.BlockSpec((None, tt, d), lambda i, j: (i, j, 0))] * n_mix,
        out_shape=[out] * n_mix,
        compiler_params=_cparams("parallel", "parallel"),
        name="rwkv_mix",
    )(us, us, us, mu)


def _stack_heads(x, head0):
    return jnp.concatenate([jnp.where(head0, x, 0.0), jnp.where(head0, 0.0, x)], axis=0)


def _split(x, parts):
    out = []
    for _ in range(parts - 1):
        hi = x.astype(BF16)
        out.append(hi)
        x = x - hi.astype(F32)
    out.append(x.astype(BF16))
    return out


def _dot_parts(a_parts, b_parts, order):
    acc = None
    for i, ai in enumerate(a_parts):
        for j, bj in enumerate(b_parts):
            if i + j < order:
                t = jnp.dot(ai, bj, preferred_element_type=F32)
                acc = t if acc is None else acc + t
    return acc


def _dot2(a, b):
    return _dot_parts(_split(a, 2), _split(b, 2), 2)


def _heads_mm(a, b, head0, wide):
    bs = _stack_heads(b, head0)
    return _dot2(a, bs) if wide else _dot(a, bs)


def _rwkv_scan_kernel(r_ref, k_ref, v_ref, a_ref, w_ref, kk_ref, ka_ref, o_ref, z_ref, *, reverse):
    tt = r_ref.shape[0]
    c = RWKV_CHUNK
    lanes = 2 * RWKV_N

    @pl.when(pl.program_id(2) == 0)
    def _():
        z_ref[...] = jnp.zeros_like(z_ref)

    lane = lax.broadcasted_iota(jnp.int32, (1, lanes), 1)
    head0 = lane < RWKV_N
    r2 = lax.broadcasted_iota(jnp.int32, (lanes, lanes), 0)
    c2 = lax.broadcasted_iota(jnp.int32, (lanes, lanes), 1)
    same_head = (r2 < RWKV_N) == (c2 < RWKV_N)
    eye2 = (r2 == c2).astype(F32)
    tr = lax.broadcasted_iota(jnp.int32, (c, c), 0)
    tc = lax.broadcasted_iota(jnp.int32, (c, c), 1)
    tri_b = ((tr <= tc) if reverse else (tr >= tc)).astype(BF16)
    pr = lax.broadcasted_iota(jnp.int32, (2 * c, 4 * c), 0)
    pc = lax.broadcasted_iota(jnp.int32, (2 * c, 4 * c), 1) % c
    incl = (pr >= c).astype(jnp.int32)
    pt = pr % c
    pmask = (pt - incl < pc) if reverse else (pt + incl > pc)
    eye_side = (lax.broadcasted_iota(jnp.int32, (c, 2 * c), 0)
                == lax.broadcasted_iota(jnp.int32, (c, 2 * c), 1) % c).astype(F32)
    kk_scale = kk_ref[...]
    ka_scale = ka_ref[...]
    lvl = (lax.broadcasted_iota(jnp.int32, (c, 2 * c), 0)
           ^ (lax.broadcasted_iota(jnp.int32, (c, 2 * c), 1) % c))

    n_chunks = tt // c
    n_pairs = r_ref.shape[1] // lanes
    order = list(range(n_chunks - 1, -1, -1) if reverse else range(n_chunks))
    sls = [(slice(ci * c, (ci + 1) * c), slice(p * lanes, (p + 1) * lanes))
           for p in range(n_pairs) for ci in order]
    vs = [v_ref[sl] for sl in sls]
    pre = []
    for sl in sls:
        k = k_ref[sl]
        asig = a_ref[sl]
        ld = w_ref[sl]
        kk = k * kk_scale[:, sl[1]]
        kk2 = kk * kk
        sq = jnp.where(head0,
                       jnp.sum(jnp.where(head0, kk2, 0.0), axis=-1, keepdims=True),
                       jnp.sum(jnp.where(head0, 0.0, kk2), axis=-1, keepdims=True))
        kk = kk * lax.rsqrt(jnp.maximum(sq, 1e-24))
        bvec = kk * asig
        kd = k * (1.0 + (asig - 1.0) * ka_scale[:, sl[1]])
        cum = _dot_parts([tri_b], _split(ld, 3), 3)
        tot = cum[0:1, :] if reverse else cum[c - 1:c, :]
        e_neg = jnp.exp(-cum)
        e_rem = jnp.exp(tot - cum)
        pre.append(dict(at=-kk * jnp.exp(cum - ld), rt=r_ref[sl] * jnp.exp(cum),
                        bt=bvec * e_neg, kt=kd * e_neg, bc=bvec * e_rem, kc=kd * e_rem,
                        wc=jnp.exp(tot)))
    ps = []
    for q in pre:
        lhs = jnp.concatenate([q["at"], q["rt"]], axis=0)
        rhs = jnp.concatenate([_stack_heads(q["bt"], head0), _stack_heads(q["kt"], head0)], axis=0)
        ps.append(jnp.where(pmask, _dot_nt(lhs, rhs), 0.0))
    a_ab = [p[:c, :2 * c] for p in ps]
    a_ak = [p[:c, 2 * c:] for p in ps]
    p_rb = [p[c:, :2 * c] for p in ps]
    p_rk = [p[c:, 2 * c:] for p in ps]
    base = [jnp.where((lvl >> 3) == 0, a, 0.0) for a in a_ab]
    pw2 = [_heads_mm(x, x, head0, False) for x in base]
    pw4 = [_heads_mm(x, x, head0, False) for x in pw2]
    inv = [eye_side + x for x in base]
    inv = [d + _heads_mm(d, x, head0, False) for d, x in zip(inv, pw2)]
    inv = [d + _heads_mm(d, x, head0, False) for d, x in zip(inv, pw4)]
    for bit in (3, 4, 5):
        off = [jnp.where((lvl >> bit) == 1, a, 0.0) for a in a_ab]
        ed = [_heads_mm(e, d, head0, False) for e, d in zip(off, inv)]
        inv = [d + _heads_mm(d, x, head0, False) for d, x in zip(inv, ed)]
    res = [eye_side - d + _heads_mm(a, d, head0, True) for a, d in zip(a_ab, inv)]
    inv = [d + _heads_mm(d, x, head0, False) for d, x in zip(inv, res)]
    at2 = [_heads_mm(d, q["at"], head0, True) for d, q in zip(inv, pre)]
    av = [_heads_mm(a, v, head0, False) for a, v in zip(a_ak, vs)]
    u_loc = [_heads_mm(d, x, head0, True) for d, x in zip(inv, av)]
    r_eff = [q["rt"] + _heads_mm(p, x, head0, False) for q, p, x in zip(pre, p_rb, at2)]
    y_loc = [_heads_mm(pb, u, head0, False) + _heads_mm(pk, v, head0, False)
             for pb, u, pk, v in zip(p_rb, u_loc, p_rk, vs)]
    m_t = [eye2 * q["wc"] + jnp.where(same_head, _dot_tn(q["bc"], x), 0.0)
           for q, x in zip(pre, at2)]
    n_t = [jnp.where(same_head, _dot_tn(q["bc"], u) + _dot_tn(q["kc"], v), 0.0)
           for q, u, v in zip(pre, u_loc, vs)]
    for p in range(n_pairs):
        z = z_ref[p]
        for i in range(p * n_chunks, (p + 1) * n_chunks):
            o_ref[sls[i]] = _dot(r_eff[i], z) + y_loc[i]
            z = _dot2(m_t[i], z) + n_t[i]
        z_ref[p] = z


def _rwkv_scan(r, k, v, asig, logw, k_k, k_a, n_seq, reverse):
    b, n_tot, d = r.shape
    lanes = 2 * RWKV_N
    tt = _pick(n_tot - n_seq, (256, 128, 64))
    assert n_seq % tt == 0 and d % lanes == 0
    n_lat, n_all = n_seq // tt, n_tot // tt
    n_pairs = _pick(d // lanes, (4, 2, 1))
    wide = n_pairs * lanes
    tile = pl.BlockSpec((None, tt, wide),
                        lambda i, h, j: (i, _scan_tile(j, n_lat, n_all, reverse), h))
    vec = pl.BlockSpec((1, wide), lambda i, h, j: (0, h))
    return pl.pallas_call(
        functools.partial(_rwkv_scan_kernel, reverse=reverse),
        grid=(b, d // wide, n_all),
        in_specs=[tile] * 5 + [vec, vec],
        out_specs=tile,
        out_shape=jax.ShapeDtypeStruct((b, n_tot, d), F32),
        scratch_shapes=[pltpu.VMEM((n_pairs, lanes, lanes), F32)],
        compiler_params=_cparams("parallel", "parallel", "arbitrary"),
        name="rwkv_bwd" if reverse else "rwkv_fwd",
    )(r, k, v, asig, logw, k_k.reshape(1, d), k_a.reshape(1, d))


def _rwkv_read_kernel(y0_ref, y1_ref, r_ref, k_ref, v_ref, a0_ref, a1_ref, g_ref,
                      ka_ref, rk_ref, lw_ref, lb_ref, o_ref):
    lanes = 2 * RWKV_N
    r2 = lax.broadcasted_iota(jnp.int32, (lanes, lanes), 0)
    c2 = lax.broadcasted_iota(jnp.int32, (lanes, lanes), 1)
    ones_bd = ((r2 < RWKV_N) == (c2 < RWKV_N)).astype(BF16)
    head_sum = lambda t: _dot_parts(_split(t, 3), [ones_bd], 3)
    for n in range(r_ref.shape[1] // lanes):
        sl = slice(n * lanes, (n + 1) * lanes)
        y = y0_ref[:, sl] + y1_ref[:, sl]
        mean = head_sum(y) * (1.0 / RWKV_N)
        yc = y - mean
        var = head_sum(yc * yc) * (1.0 / RWKV_N)
        y = yc * lax.rsqrt(var + RWKV_GN_EPS) * lw_ref[:, sl] + lb_ref[:, sl]
        r = r_ref[:, sl]
        k = k_ref[:, sl]
        ka = ka_ref[:, sl]
        k0 = k * (1.0 + (a0_ref[:, sl] - 1.0) * ka)
        k1 = k * (1.0 + (a1_ref[:, sl] - 1.0) * ka)
        rk = rk_ref[:, sl]
        bonus = head_sum(r * k0 * rk + r * k1 * rk) * v_ref[:, sl]
        o_ref[:, sl] = ((y + bonus) * g_ref[:, sl]).astype(o_ref.dtype)


def _rwkv_read(y0, y1, r, k, v, a0, a1, g, k_a, r_k, ln_w, ln_b):
    m, d = r.shape
    tm = _pick(m, (128, 64, 32, 16, 8))
    tile = pl.BlockSpec((tm, d), lambda i: (i, 0))
    vec = pl.BlockSpec((1, d), lambda i: (0, 0))
    return pl.pallas_call(
        _rwkv_read_kernel,
        grid=(m // tm,),
        in_specs=[tile] * 8 + [vec] * 4,
        out_specs=tile,
        out_shape=jax.ShapeDtypeStruct((m, d), BF16),
        compiler_params=_cparams("parallel"),
        name="rwkv_readout",
    )(y0, y1, r, k, v, a0, a1, g, k_a.reshape(1, d), r_k.reshape(1, d),
      ln_w.reshape(1, d), ln_b.reshape(1, d))


def _pad_rows(w2, dr):
    zeros = jnp.zeros_like(w2[0])
    return jnp.concatenate([w2[0], zeros] if dr == 0 else [zeros, w2[1]], axis=0)


def _rwkv_mixer(us, n_seq, j, mu, w_rkv, w0, w1, w2, a0, a1, a2, g1, g2, k_k, k_a, r_k,
                ln_w, ln_b, w_out):
    b, n_tot, d = us.shape
    m = b * n_tot
    xr, xw, xk, xv, xa, xg = [t.reshape(m, d) for t in _rwkv_mix(us, mu, n_seq)]
    r = _mm(xr, w_rkv, widx=(j, 0))
    k = _mm(xk, w_rkv, widx=(j, 1))
    v = _mm(xv, w_rkv, widx=(j, 2))
    w_low = _mm(xw, jnp.concatenate([w1[0], w1[1]], axis=-1), act="tanh")
    a_low = _mm(xa, jnp.concatenate([a1[0], a1[1]], axis=-1))
    g_low = _mm(xg, g1, act="sigmoid")
    gate = _mm(g_low, g2)
    sh = (b, n_tot, d)
    ys, asigs = [], []
    for dr in range(2):
        logw = _mm(w_low, _pad_rows(w2, dr), w0[dr], act="rwkv_decay")
        asig = _mm(a_low, _pad_rows(a2, dr), a0[dr], act="sigmoid")
        asigs.append(asig)
        ys.append(_rwkv_scan(r.reshape(sh), k.reshape(sh), v.reshape(sh), asig.reshape(sh),
                             logw.reshape(sh), k_k, k_a, n_seq, dr == 1).reshape(m, d))
    hy = _rwkv_read(ys[0], ys[1], r, k, v, asigs[0], asigs[1], gate, k_a, r_k.reshape(-1),
                    ln_w, ln_b)
    return _mm(hy, w_out, widx=(j,)).reshape(b, n_tot, d)


def kernel(x, c, ctx, c_ctx, ada_w, ada_b, norm_g, ffn_w_in, ffn_w_out, lru_w_in, lru_b_in, lru_conv_w, lru_conv_b, lru_gate_w, lru_gate_b, lru_log_lambda, lru_w_out, lru_b_out, gla_w_in, gla_b_r, gla_gate_w1, gla_gate_w2, gla_gate_b, gla_norm_g, gla_w_out, rwkv_mu, rwkv_w_rkv, rwkv_w0, rwkv_w1, rwkv_w2, rwkv_a0, rwkv_a1, rwkv_a2, rwkv_g1, rwkv_g2, rwkv_k_k, rwkv_k_a, rwkv_r_k, rwkv_ln_w, rwkv_ln_b, rwkv_w_out):
    b, n_seq, d = x.shape
    depth = ada_w.shape[0]
    assert b < 8
    cond = jnp.concatenate([c, c_ctx[None, :], jnp.zeros((8 - b - 1, d), F32)], axis=0)
    mods = _mods(cond, ada_w, ada_b).reshape(depth, 8, N_MOD, d)
    h_lat, h_ctx = x, ctx
    for i in range(depth):
        ctx_out = i < depth - 1
        kind, j = i % N_MIXERS, i // N_MIXERS
        col_major = i % 2 == 1
        m_lat, m_ctx = mods[i, :b], mods[i, b]
        us = _norm_mod(h_lat, h_ctx, norm_g[i, 0], m_lat, m_ctx, 0, 1, col_major,
                       F32 if kind == 2 else BF16)
        if kind == 0:
            y = _rglru_mixer(us, n_seq, j, lru_w_in, lru_b_in[j], lru_conv_w[j], lru_conv_b[j],
                             lru_gate_w[j], lru_gate_b[j], lru_log_lambda[j], lru_w_out,
                             lru_b_out[j])
        elif kind == 1:
            y = _gla_mixer(us, n_seq, j, gla_w_in, gla_b_r[j], gla_gate_w1[j], gla_gate_w2[j],
                           gla_gate_b[j], gla_norm_g[j], gla_w_out)
        else:
            y = _rwkv_mixer(us, n_seq, j, rwkv_mu[j], rwkv_w_rkv, rwkv_w0[j], rwkv_w1[j],
                            rwkv_w2[j], rwkv_a0[j], rwkv_a1[j], rwkv_a2[j], rwkv_g1[j],
                            rwkv_g2[j], rwkv_k_k[j], rwkv_k_a[j], rwkv_r_k[j], rwkv_ln_w[j],
                            rwkv_ln_b[j], rwkv_w_out)
        h_lat, h_ctx, us = _resid(h_lat, h_ctx if ctx_out else None, y, norm_g[i, 1], m_lat,
                                  m_ctx, 2, col_major, follow=(3, 4), g2=norm_g[i, 2])
        n_tot = us.shape[1]
        mid = _ffn_in(us.reshape(b * n_tot, d), ffn_w_in, i)
        y = _mm(mid, ffn_w_out, widx=(i,)).reshape(b, n_tot, d)
        h_lat, h_ctx, _ = _resid(h_lat, h_ctx, y, norm_g[i, 3], m_lat, m_ctx, 5, col_major)
    return h_lat
```

```python
import functools

import jax
import jax.numpy as jnp
from jax import lax
from jax.experimental import pallas as pl
from jax.experimental.pallas import tpu as pltpu

F32 = jnp.float32
BF16 = jnp.bfloat16

GRID_W = 64
N_MIXERS = 3
N_MOD = 6
NORM_EPS = 1e-6
LRU_BLOCKS = 16
LRU_C = 8.0
GLA_HEADS = 4
GLA_TAU = 16.0
GLA_CHUNK = 64
RWKV_N = 64
RWKV_CHUNK = 64
RWKV_GN_EPS = 64e-5
LANES = 128
SUBLANES = 8
VMEM_LIMIT = 52 * 1024 * 1024
PROJ_VMEM_BUDGET = 44 * 1024 * 1024


def _cparams(*sem):
    return pltpu.CompilerParams(dimension_semantics=sem, vmem_limit_bytes=VMEM_LIMIT)


def _pick(n, cands):
    for c in cands:
        if n % c == 0:
            return c
    return n


def _dot(a, b):
    return jnp.dot(a.astype(BF16), b.astype(BF16), preferred_element_type=F32)


def _dot_nt(a, b):
    return lax.dot_general(a.astype(BF16), b.astype(BF16), (((1,), (1,)), ((), ())),
                           preferred_element_type=F32)


def _dot_tn(a, b):
    return lax.dot_general(a.astype(BF16), b.astype(BF16), (((0,), (0,)), ((), ())),
                           preferred_element_type=F32)


def _sigmoid(x):
    return 1.0 / (1.0 + jnp.exp(-x))


def _softplus(x):
    return jnp.maximum(x, 0.0) + jnp.log(1.0 + jnp.exp(-jnp.abs(x)))


def _mods_kernel(x_ref, w_ref, b_ref, o_ref):
    x = x_ref[...]
    x = x * _sigmoid(x)
    o_ref[...] = _dot(x, w_ref[...]) + b_ref[...]


def _mods(cond, ada_w, ada_b):
    n_layers, d, n = ada_w.shape
    tn = _pick(n, (1024, 512, 256, 128))
    return pl.pallas_call(
        _mods_kernel,
        grid=(n_layers, n // tn),
        in_specs=[pl.BlockSpec((8, d), lambda l, j: (0, 0)),
                  pl.BlockSpec((None, d, tn), lambda l, j: (l, 0, j)),
                  pl.BlockSpec((None, 1, tn), lambda l, j: (l, 0, j))],
        out_specs=pl.BlockSpec((None, 8, tn), lambda l, j: (l, 0, j)),
        out_shape=jax.ShapeDtypeStruct((n_layers, 8, n), F32),
        compiler_params=_cparams("parallel", "parallel"),
        name="ada_mods",
    )(cond, ada_w, ada_b.reshape(n_layers, 1, n))


def _norm_mod_kernel(*refs, k_shift, k_scale, n_lat):
    def emit(h_ref, m_ref, o_ref):
        x = h_ref[...]
        y = x * lax.rsqrt(jnp.mean(x * x, axis=-1, keepdims=True) + NORM_EPS) * g_ref[...]
        y = y * (1.0 + m_ref[k_scale:k_scale + 1, :]) + m_ref[k_shift:k_shift + 1, :]
        o_ref[...] = y.astype(o_ref.dtype)

    if n_lat is None:
        h_ref, g_ref, m_ref, o_ref = refs
        emit(h_ref, m_ref, o_ref)
        return
    h_ref, c_ref, g_ref, m_ref, mc_ref, o_ref = refs
    is_ctx = pl.program_id(1) >= n_lat

    @pl.when(jnp.logical_not(is_ctx))
    def _():
        emit(h_ref, m_ref, o_ref)

    @pl.when(is_ctx)
    def _():
        emit(c_ref, mc_ref, o_ref)


def _norm_mod(h_lat, h_ctx, g, m_lat, m_ctx, k_shift, k_scale, col_major, out_dtype):
    b, n, d = h_lat.shape
    rows = n // GRID_W
    n_ctx = 0 if h_ctx is None else h_ctx.shape[1]
    g = g.reshape(1, d)
    if col_major:
        tt = rows
        n_lat = GRID_W
        h_in = h_lat.reshape(b, rows, GRID_W * d)
        h_spec = pl.BlockSpec((None, rows, d), lambda i, j: (i, 0, jnp.minimum(j, n_lat - 1)))
    else:
        tt = _pick(n_ctx if n_ctx else n, (256, 128, 64, 32, 16, 8))
        n_lat = n // tt
        h_in = h_lat
        h_spec = pl.BlockSpec((None, tt, d), lambda i, j: (i, jnp.minimum(j, n_lat - 1), 0))
    assert n_ctx % tt == 0 and n % tt == 0
    g_spec = pl.BlockSpec((1, d), lambda i, j: (0, 0))
    m_spec = pl.BlockSpec((None, N_MOD, d), lambda i, j: (i, 0, 0))
    if h_ctx is None:
        in_specs, args = [h_spec, g_spec, m_spec], (h_in, g, m_lat)
    else:
        in_specs = [h_spec,
                    pl.BlockSpec((None, tt, d), lambda i, j: (i, jnp.maximum(j - n_lat, 0), 0)),
                    g_spec, m_spec,
                    pl.BlockSpec((None, N_MOD, d), lambda i, j: (0, 0, 0))]
        args = (h_in, h_ctx, g, m_lat, m_ctx.reshape(1, N_MOD, d))
    return pl.pallas_call(
        functools.partial(_norm_mod_kernel, k_shift=k_shift, k_scale=k_scale,
                          n_lat=None if h_ctx is None else n_lat),
        grid=(b, n_lat + n_ctx // tt),
        in_specs=in_specs,
        out_specs=pl.BlockSpec((None, tt, d), lambda i, j: (i, j, 0)),
        out_shape=jax.ShapeDtypeStruct((b, n + n_ctx, d), out_dtype),
        compiler_params=_cparams("parallel", "arbitrary"),
        name="norm_mod",
    )(*args)


def _resid_kernel(*refs, k_gate, follow, n_lat):
    it = iter(refs)
    both = n_lat is not None
    h_ref = next(it)
    c_ref = next(it) if both else None
    y_ref, g_ref = next(it), next(it)
    g2_ref = next(it) if follow else None
    m_ref = next(it)
    mc_ref = next(it) if both else None
    oh_ref = next(it)
    oc_ref = next(it) if both else None
    ou_ref = next(it) if follow else None

    def emit(src, mods, dst):
        y = y_ref[...]
        yn = y * lax.rsqrt(jnp.mean(y * y, axis=-1, keepdims=True) + NORM_EPS) * g_ref[...]
        hn = src[...] + mods[k_gate:k_gate + 1, :] * yn
        dst[...] = hn
        if follow:
            k_shift, k_scale = follow
            u = hn * lax.rsqrt(jnp.mean(hn * hn, axis=-1, keepdims=True) + NORM_EPS) * g2_ref[...]
            u = u * (1.0 + mods[k_scale:k_scale + 1, :]) + mods[k_shift:k_shift + 1, :]
            ou_ref[...] = u.astype(ou_ref.dtype)

    if not both:
        emit(h_ref, m_ref, oh_ref)
        return
    is_ctx = pl.program_id(1) >= n_lat

    @pl.when(jnp.logical_not(is_ctx))
    def _():
        emit(h_ref, m_ref, oh_ref)

    @pl.when(is_ctx)
    def _():
        emit(c_ref, mc_ref, oc_ref)


def _resid(h_lat, h_ctx, y, g, m_lat, m_ctx, k_gate, col_major, follow=None, g2=None):
    b, n, d = h_lat.shape
    rows = n // GRID_W
    n_ctx = 0 if h_ctx is None else h_ctx.shape[1]
    both = h_ctx is not None
    if col_major:
        tt = rows
        n_lat = GRID_W
        h_in = h_lat.reshape(b, rows, GRID_W * d)
        h_spec = pl.BlockSpec((None, rows, d), lambda i, j: (i, 0, jnp.minimum(j, n_lat - 1)))
        lat_shape = jax.ShapeDtypeStruct((b, rows, GRID_W * d), F32)
    else:
        tt = _pick(n_ctx if n_ctx else n, (256, 128, 64, 32, 16, 8))
        n_lat = n // tt
        h_in = h_lat
        h_spec = pl.BlockSpec((None, tt, d), lambda i, j: (i, jnp.minimum(j, n_lat - 1), 0))
        lat_shape = jax.ShapeDtypeStruct((b, n, d), F32)
    assert n_ctx % tt == 0 and n % tt == 0
    c_spec = pl.BlockSpec((None, tt, d), lambda i, j: (i, jnp.maximum(j - n_lat, 0), 0))
    y_spec = pl.BlockSpec((None, tt, d), lambda i, j: (i, j, 0))
    g_spec = pl.BlockSpec((1, d), lambda i, j: (0, 0))
    in_specs, args = [h_spec], [h_in]
    if both:
        in_specs.append(c_spec)
        args.append(h_ctx)
    in_specs += [y_spec, g_spec]
    args += [y, g.reshape(1, d)]
    if follow:
        in_specs.append(g_spec)
        args.append(g2.reshape(1, d))
    in_specs.append(pl.BlockSpec((None, N_MOD, d), lambda i, j: (i, 0, 0)))
    args.append(m_lat)
    out_specs, out_shape = [h_spec], [lat_shape]
    if both:
        in_specs.append(pl.BlockSpec((None, N_MOD, d), lambda i, j: (0, 0, 0)))
        args.append(m_ctx.reshape(1, N_MOD, d))
        out_specs.append(c_spec)
        out_shape.append(jax.ShapeDtypeStruct((b, n_ctx, d), F32))
    if follow:
        out_specs.append(y_spec)
        out_shape.append(jax.ShapeDtypeStruct((b, n + n_ctx, d), BF16))
    outs = pl.pallas_call(
        functools.partial(_resid_kernel, k_gate=k_gate, follow=follow,
                          n_lat=n_lat if both else None),
        grid=(b, n_lat + n_ctx // tt),
        in_specs=in_specs,
        out_specs=out_specs,
        out_shape=out_shape,
        compiler_params=_cparams("parallel", "arbitrary"),
        name="resid",
    )(*args)
    outs = list(outs)
    new_lat = outs.pop(0).reshape(b, n, d)
    new_ctx = outs.pop(0) if both else None
    return new_lat, new_ctx, (outs.pop(0) if follow else None)


def _gelu_tanh(x):
    return 0.5 * x * (1.0 + jnp.tanh(0.7978845608028654 * (x + 0.044715 * x * x * x)))


def _apply_act(z, act):
    if act is None:
        return z
    if act == "gelu":
        return _gelu_tanh(z)
    if act == "tanh":
        return jnp.tanh(z)
    if act == "sigmoid":
        return _sigmoid(z)
    if act == "gla_gate":
        return -_softplus(-z) * (1.0 / GLA_TAU)
    if act == "rwkv_decay":
        return -jnp.exp(-_softplus(-z) - 0.5)
    raise ValueError(act)


def _proj_tiles(m, k, n, x_bytes, o_bytes, n_weights):
    best = None
    for tm in (2048, 1536, 1024, 512, 256, 128, 64, 32, 16, 8):
        if m % tm:
            continue
        for tn in (1024, 512, 256, 128, n):
            if n % tn:
                continue
            need = (2 * tm * k * x_bytes + n_weights * k * tn * (2 * 4 + 2)
                    + 2 * tm * tn * o_bytes + (tm * tn * 4 if o_bytes < 4 else 0))
            if need <= PROJ_VMEM_BUDGET and (best is None or tm * tn > best[0] * best[1]):
                best = (tm, tn)
    assert best is not None
    return best


def _row_chunks(tm, rows=512):
    rows = min(rows, tm)
    return [slice(r, r + rows) for r in range(0, tm, rows)]


def _mm_kernel(x_ref, w_ref, b_ref, o_ref, wb, *, act):
    @pl.when(pl.program_id(1) == 0)
    def _():
        wb[...] = w_ref[...].astype(BF16)

    for rows in _row_chunks(x_ref.shape[0]):
        z = _dot(x_ref[rows, :], wb[...]) + b_ref[...]
        o_ref[rows, :] = _apply_act(z, act).astype(o_ref.dtype)


def _mm(x, w, bias=None, act=None, out_dtype=F32, col0=0, ncols=None, widx=()):
    m, k = x.shape
    n_total = w.shape[-1]
    ncols = n_total if ncols is None else ncols
    tm, tn = _proj_tiles(m, k, ncols, x.dtype.itemsize, jnp.dtype(out_dtype).itemsize, 1)
    assert col0 % tn == 0 and w.ndim == 2 + len(widx)
    off = col0 // tn
    if bias is None:
        bias = jnp.zeros((ncols,), F32)
    return pl.pallas_call(
        functools.partial(_mm_kernel, act=act),
        grid=(ncols // tn, m // tm),
        in_specs=[pl.BlockSpec((tm, k), lambda j, i: (i, 0)),
                  pl.BlockSpec((None,) * len(widx) + (k, tn), lambda j, i: widx + (0, off + j)),
                  pl.BlockSpec((1, tn), lambda j, i: (0, j))],
        out_specs=pl.BlockSpec((tm, tn), lambda j, i: (i, j)),
        out_shape=jax.ShapeDtypeStruct((m, ncols), out_dtype),
        scratch_shapes=[pltpu.VMEM((k, tn), BF16)],
        compiler_params=_cparams("parallel", "arbitrary"),
        name="proj",
    )(x, w, bias.reshape(1, ncols).astype(F32))


def _ffn_in_kernel(x_ref, wg_ref, wu_ref, o_ref, wgb, wub):
    @pl.when(pl.program_id(1) == 0)
    def _():
        wgb[...] = wg_ref[...].astype(BF16)
        wub[...] = wu_ref[...].astype(BF16)

    for rows in _row_chunks(x_ref.shape[0]):
        x = x_ref[rows, :]
        gate = _dot(x, wgb[...])
        up = _dot(x, wub[...])
        o_ref[rows, :] = (gate * _sigmoid(gate) * up).astype(o_ref.dtype)


def _ffn_in(x, w_in, layer):
    m, k = x.shape
    d_ff = w_in.shape[-1] // 2
    tm, tn = _proj_tiles(m, k, d_ff, x.dtype.itemsize, 2, 2)
    off = d_ff // tn
    return pl.pallas_call(
        _ffn_in_kernel,
        grid=(d_ff // tn, m // tm),
        in_specs=[pl.BlockSpec((tm, k), lambda j, i: (i, 0)),
                  pl.BlockSpec((None, k, tn), lambda j, i: (layer, 0, j)),
                  pl.BlockSpec((None, k, tn), lambda j, i: (layer, 0, off + j))],
        out_specs=pl.BlockSpec((tm, tn), lambda j, i: (i, j)),
        out_shape=jax.ShapeDtypeStruct((m, d_ff), BF16),
        scratch_shapes=[pltpu.VMEM((k, tn), BF16)] * 2,
        compiler_params=_cparams("parallel", "arbitrary"),
        name="ffn_in",
    )(x, w_in, w_in)


def _scan_tile(step, n_lat, n_all, reverse):
    if reverse:
        return n_all - 1 - step
    n_ctx = n_all - n_lat
    return jnp.where(step < n_ctx, n_lat + step, step - n_ctx)


def _lru_conv_kernel(x_ref, p_ref, n_ref, w_ref, b_ref, o_ref, buf, *, n_lat, n_all):
    t = pl.program_id(1)
    tt = x_ref.shape[0]
    first = jnp.logical_or(t == 0, t == n_lat)
    last = jnp.logical_or(t == n_lat - 1, t == n_all - 1)
    buf[0:SUBLANES, :] = jnp.where(first, 0.0, p_ref[...])
    buf[SUBLANES:SUBLANES + tt, :] = x_ref[...]
    buf[SUBLANES + tt:2 * SUBLANES + tt, :] = jnp.where(last, 0.0, n_ref[...])
    acc = b_ref[...] + w_ref[0:1, :] * buf[SUBLANES - 2:SUBLANES - 2 + tt, :]
    acc = acc + w_ref[1:2, :] * buf[SUBLANES - 1:SUBLANES - 1 + tt, :]
    acc = acc + w_ref[2:3, :] * buf[SUBLANES:SUBLANES + tt, :]
    acc = acc + w_ref[3:4, :] * buf[SUBLANES + 1:SUBLANES + 1 + tt, :]
    o_ref[...] = acc


def _lru_conv(rec, conv_w, conv_b, n_seq):
    b, n_tot, w = rec.shape
    tt = _pick(n_tot - n_seq, (256, 128, 64, 32, 16, 8))
    assert n_seq % tt == 0 and conv_w.shape[0] == 4
    n_lat, n_all = n_seq // tt, n_tot // tt
    r8 = tt // SUBLANES
    last8 = n_tot // SUBLANES - 1
    return pl.pallas_call(
        functools.partial(_lru_conv_kernel, n_lat=n_lat, n_all=n_all),
        grid=(b, n_all),
        in_specs=[pl.BlockSpec((None, tt, w), lambda i, j: (i, j, 0)),
                  pl.BlockSpec((None, SUBLANES, w), lambda i, j: (i, jnp.maximum(j * r8 - 1, 0), 0)),
                  pl.BlockSpec((None, SUBLANES, w), lambda i, j: (i, jnp.minimum((j + 1) * r8, last8), 0)),
                  pl.BlockSpec((4, w), lambda i, j: (0, 0)),
                  pl.BlockSpec((1, w), lambda i, j: (0, 0))],
        out_specs=pl.BlockSpec((None, tt, w), lambda i, j: (i, j, 0)),
        out_shape=jax.ShapeDtypeStruct((b, n_tot, w), F32),
        scratch_shapes=[pltpu.VMEM((tt + 2 * SUBLANES, w), F32)],
        compiler_params=_cparams("parallel", "parallel"),
        name="lru_conv",
    )(rec, rec, rec, conv_w, conv_b.reshape(1, w))


def _lru_scan_kernel(*refs, reverse, combine, blk):
    if combine:
        x_ref, wc_ref, gb_ref, ll_ref, hf_ref, gate_ref, o_ref, a_s, b_s, carry = refs
    else:
        x_ref, wc_ref, gb_ref, ll_ref, o_ref, a_s, b_s, carry = refs
    tt, w = x_ref.shape
    nlb = w // LANES

    @pl.when(pl.program_id(1) == 0)
    def _():
        carry[...] = jnp.zeros_like(carry)

    def put(dst, n, val):
        if blk >= LANES:
            for p in range(blk // LANES):
                dst[n * (blk // LANES) + p] = val[:, p * LANES:(p + 1) * LANES]
        else:
            off = (n * blk) % LANES
            dst[(n * blk) // LANES, :, off:off + blk] = val

    sp = _softplus(-ll_ref[...])
    for n in range(w // blk):
        sl = slice(n * blk, (n + 1) * blk)
        xb = x_ref[:, sl]
        gates = _dot(xb, wc_ref[n])
        rr = _sigmoid(gates[:, :blk] + gb_ref[0:1, sl])
        ii = _sigmoid(gates[:, blk:] + gb_ref[1:2, sl])
        a = jnp.exp(-LRU_C * rr * sp[:, sl])
        put(a_s, n, a)
        om = 1.0 - a * a
        put(b_s, n, om * lax.rsqrt(jnp.maximum(om, 1e-30)) * (ii * xb))

    groups = tt // SUBLANES
    row = lax.broadcasted_iota(jnp.int32, (groups, SUBLANES, LANES), 1)
    for n in range(nlb):
        a = a_s[n].reshape(groups, SUBLANES, LANES)
        bt = b_s[n].reshape(groups, SUBLANES, LANES)
        for s in (1, 2, 4):
            keep = (row < SUBLANES - s) if reverse else (row >= s)
            shift = SUBLANES - s if reverse else s
            a_sh = jnp.where(keep, pltpu.roll(a, shift, 1), 1.0)
            b_sh = jnp.where(keep, pltpu.roll(bt, shift, 1), 0.0)
            bt = a * b_sh + bt
            a = a * a_sh
        c = carry[n]
        outs = [None] * groups
        for g in (range(groups - 1, -1, -1) if reverse else range(groups)):
            h = a[g] * c + bt[g]
            outs[g] = h
            c = h[0:1, :] if reverse else h[SUBLANES - 1:SUBLANES, :]
        carry[n] = c
        hl = jnp.concatenate(outs, axis=0)
        sl = slice(n * LANES, (n + 1) * LANES)
        if combine:
            o_ref[:, sl] = ((hl + hf_ref[:, sl]) * gate_ref[:, sl]).astype(o_ref.dtype)
        else:
            o_ref[:, sl] = hl


def _lru_scan(x, wcat, gate_b, log_lambda, n_seq, reverse, h_fwd=None, gate=None):
    b, n_tot, w = x.shape
    blk = w // LRU_BLOCKS
    tt = _pick(n_tot - n_seq, (256, 128, 64))
    assert n_seq % tt == 0 and tt % (SUBLANES * SUBLANES) == 0
    n_lat, n_all = n_seq // tt, n_tot // tt
    combine = h_fwd is not None
    tile = lambda i, j: (i, _scan_tile(j, n_lat, n_all, reverse), 0)
    in_specs = [pl.BlockSpec((None, tt, w), tile),
                pl.BlockSpec((LRU_BLOCKS, blk, 2 * blk), lambda i, j: (0, 0, 0)),
                pl.BlockSpec((2, w), lambda i, j: (0, 0)),
                pl.BlockSpec((1, w), lambda i, j: (0, 0))]
    args = [x, wcat, gate_b, log_lambda.reshape(1, w)]
    if combine:
        in_specs += [pl.BlockSpec((None, tt, w), tile), pl.BlockSpec((None, tt, w), tile)]
        args += [h_fwd, gate]
    return pl.pallas_call(
        functools.partial(_lru_scan_kernel, reverse=reverse, combine=combine, blk=blk),
        grid=(b, n_all),
        in_specs=in_specs,
        out_specs=pl.BlockSpec((None, tt, w), tile),
        out_shape=jax.ShapeDtypeStruct((b, n_tot, w), BF16 if combine else F32),
        scratch_shapes=([pltpu.VMEM((w // LANES, tt, LANES), F32)] * 2
                        + [pltpu.VMEM((w // LANES, 1, LANES), F32)]),
        compiler_params=_cparams("parallel", "arbitrary"),
        name="lru_scan_bwd" if reverse else "lru_scan_fwd",
    )(*args)


def _rglru_mixer(us, n_seq, j, w_in, b_in, conv_w, conv_b, gate_w, gate_b, log_lambda, w_out, b_out):
    b, n_tot, d = us.shape
    w = w_in.shape[-1] // 2
    x2 = us.reshape(b * n_tot, d)
    gate = _mm(x2, w_in, b_in[:w], act="gelu", col0=0, ncols=w, widx=(j,)).reshape(b, n_tot, w)
    rec = _mm(x2, w_in, b_in[w:], col0=w, ncols=w, widx=(j,)).reshape(b, n_tot, w)
    x = _lru_conv(rec, conv_w, conv_b, n_seq)
    wcat = jnp.concatenate([gate_w[:, 0], gate_w[:, 1]], axis=-1).astype(BF16)
    h_f = _lru_scan(x, wcat[0], gate_b[0], log_lambda[0], n_seq, False)
    hy = _lru_scan(x, wcat[1], gate_b[1], log_lambda[1], n_seq, True, h_fwd=h_f, gate=gate)
    y = _mm(hy.reshape(b * n_tot, w), w_out, b_out, widx=(j,))
    return y.reshape(b, n_tot, d)


def _gla_kernel(*refs, reverse, combine, scale):
    if combine:
        q_ref, k_ref, v_ref, g_ref, of_ref, r_ref, br_ref, ng_ref, o_ref, st = refs
    else:
        q_ref, k_ref, v_ref, g_ref, o_ref, st = refs
    tt = q_ref.shape[0]
    c = GLA_CHUNK

    @pl.when(pl.program_id(2) == 0)
    def _():
        st[...] = jnp.zeros_like(st)

    row = lax.broadcasted_iota(jnp.int32, (c, c), 0)
    col = lax.broadcasted_iota(jnp.int32, (c, c), 1)
    tri = (row <= col) if reverse else (row >= col)
    tri_b = tri.astype(BF16)
    n_chunks = tt // c
    n_heads, hv, hk = st.shape
    work = []
    for p in range(n_heads):
        lk = slice(p * hk, (p + 1) * hk)
        lv = slice(p * hv, (p + 1) * hv)
        for ci in (range(n_chunks - 1, -1, -1) if reverse else range(n_chunks)):
            sl = slice(ci * c, (ci + 1) * c)
            k = k_ref[sl, lk]
            v = v_ref[sl, lv].astype(BF16)
            bcum = _dot_parts([tri_b], _split(g_ref[sl, lk], 3), 3)
            b_last = bcum[0:1, :] if reverse else bcum[c - 1:c, :]
            q_e = (q_ref[sl, lk] * scale * jnp.exp(bcum)).astype(BF16)
            k_e = k * jnp.exp(-bcum)
            att = jnp.where(tri, _dot_nt(q_e, k_e), 0.0)
            o = _dot(att, v)
            if combine:
                o = o + of_ref[sl, lv]
            kv = _dot_tn(v, k * jnp.exp(b_last - bcum))
            work.append((sl, lv, q_e, o, jnp.exp(b_last), kv))
    for p in range(n_heads):
        s_t = st[p]
        for sl, lv, q_e, o, decay, kv in work[p * n_chunks:(p + 1) * n_chunks]:
            o = o + _dot_nt(q_e, s_t)
            if combine:
                o = o * lax.rsqrt(jnp.mean(o * o, axis=-1, keepdims=True) + NORM_EPS) * ng_ref[:, lv]
                r = r_ref[sl, lv] + br_ref[:, lv]
                o = o * (r * _sigmoid(r))
            o_ref[sl, lv] = o.astype(o_ref.dtype)
            s_t = s_t * decay + kv
        st[p] = s_t


def _gla_scan(qkvr, g, n_seq, reverse, o_fwd=None, b_r=None, norm_g=None):
    b, n_tot, _ = qkvr.shape
    dk = g.shape[-1]
    hk = dk // GLA_HEADS
    hv = 2 * hk
    tt = _pick(n_tot - n_seq, (256, 128, 64))
    assert n_seq % tt == 0
    n_lat, n_all = n_seq // tt, n_tot // tt
    combine = o_fwd is not None
    hps = 2
    wk, wv = hps * hk, hps * hv
    kq, kv = dk // wk, 2 * dk // wv
    tile = lambda i, h, j: _scan_tile(j, n_lat, n_all, reverse)
    in_specs = [pl.BlockSpec((None, tt, wk), lambda i, h, j: (i, tile(i, h, j), h)),
                pl.BlockSpec((None, tt, wk), lambda i, h, j: (i, tile(i, h, j), kq + h)),
                pl.BlockSpec((None, tt, wv), lambda i, h, j: (i, tile(i, h, j), kv + h)),
                pl.BlockSpec((None, tt, wk), lambda i, h, j: (i, tile(i, h, j), h))]
    args = [qkvr, qkvr, qkvr, g]
    o_spec = pl.BlockSpec((None, tt, wv), lambda i, h, j: (i, tile(i, h, j), h))
    if combine:
        kr = 4 * dk // wv
        vec = pl.BlockSpec((1, wv), lambda i, h, j: (0, h))
        in_specs += [o_spec, pl.BlockSpec((None, tt, wv), lambda i, h, j: (i, tile(i, h, j), kr + h)),
                     vec, vec]
        args += [o_fwd, qkvr, b_r.reshape(1, 2 * dk), norm_g.reshape(1, 2 * dk)]
    return pl.pallas_call(
        functools.partial(_gla_kernel, reverse=reverse, combine=combine, scale=float(hk) ** -0.5),
        grid=(b, GLA_HEADS // hps, n_all),
        in_specs=in_specs,
        out_specs=o_spec,
        out_shape=jax.ShapeDtypeStruct((b, n_tot, 2 * dk), BF16 if combine else F32),
        scratch_shapes=[pltpu.VMEM((hps, hv, hk), F32)],
        compiler_params=_cparams("parallel", "parallel", "arbitrary"),
        name="gla_bwd" if reverse else "gla_fwd",
    )(*args)


def _gla_mixer(us, n_seq, j, w_in, b_r, gate_w1, gate_w2, gate_b, norm_g, w_out):
    b, n_tot, d = us.shape
    m = b * n_tot
    x2 = us.reshape(m, d)
    dk = gate_w2.shape[-1]
    qkvr = _mm(x2, w_in, widx=(j,))
    low = _mm(x2, jnp.concatenate([gate_w1[0], gate_w1[1]], axis=-1))
    zeros = jnp.zeros_like(gate_w2[0])
    o = None
    for dr in range(2):
        w2 = jnp.concatenate([gate_w2[0], zeros] if dr == 0 else [zeros, gate_w2[1]], axis=0)
        g = _mm(low, w2, gate_b[dr], act="gla_gate")
        o = _gla_scan(qkvr.reshape(b, n_tot, -1), g.reshape(b, n_tot, dk), n_seq, dr == 1,
                      o_fwd=o, b_r=b_r, norm_g=norm_g)
    return _mm(o.reshape(m, 2 * dk), w_out, widx=(j,)).reshape(b, n_tot, d)


def _rwkv_mix_kernel(x_ref, p_ref, n_ref, mu_ref, *o_refs, n_lat, n_all):
    t = pl.program_id(1)
    tt = x_ref.shape[0]
    first = jnp.logical_or(t == 0, t == n_lat)
    last = jnp.logical_or(t == n_lat - 1, t == n_all - 1)
    x = x_ref[...]
    prev_row = jnp.where(first, 0.0, p_ref[SUBLANES - 1:SUBLANES, :])
    next_row = jnp.where(last, 0.0, n_ref[0:1, :])
    rid = lax.broadcasted_iota(jnp.int32, x.shape, 0)
    up = jnp.where(rid == 0, prev_row, pltpu.roll(x, 1, 0))
    dn = jnp.where(rid == tt - 1, next_row, pltpu.roll(x, tt - 1, 0))
    dx = 0.5 * (up + dn) - x
    for i, o_ref in enumerate(o_refs):
        o_ref[...] = (x + dx * mu_ref[i:i + 1, :]).astype(o_ref.dtype)


def _rwkv_mix(us, mu, n_seq):
    b, n_tot, d = us.shape
    n_mix = mu.shape[0]
    tt = _pick(n_tot - n_seq, (256, 128, 64, 32, 16, 8))
    assert n_seq % tt == 0
    n_lat, n_all = n_seq // tt, n_tot // tt
    r8 = tt // SUBLANES
    last8 = n_tot // SUBLANES - 1
    out = jax.ShapeDtypeStruct((b, n_tot, d), BF16)
    return pl.pallas_call(
        functools.partial(_rwkv_mix_kernel, n_lat=n_lat, n_all=n_all),
        grid=(b, n_all),
        in_specs=[pl.BlockSpec((None, tt, d), lambda i, j: (i, j, 0)),
                  pl.BlockSpec((None, SUBLANES, d), lambda i, j: (i, jnp.maximum(j * r8 - 1, 0), 0)),
                  pl.BlockSpec((None, SUBLANES, d), lambda i, j: (i, jnp.minimum((j + 1) * r8, last8), 0)),
                  pl.BlockSpec((n_mix, d), lambda i, j: (0, 0))],
        out_specs=[pl.BlockSpec((None, tt, d), lambda i, j: (i, j, 0))] * n_mix,
        out_shape=[out] * n_mix,
        compiler_params=_cparams("parallel", "parallel"),
        name="rwkv_mix",
    )(us, us, us, mu)


def _stack_heads(x, head0):
    return jnp.concatenate([jnp.where(head0, x, 0.0), jnp.where(head0, 0.0, x)], axis=0)


def _split(x, parts):
    out = []
    for _ in range(parts - 1):
        hi = x.astype(BF16)
        out.append(hi)
        x = x - hi.astype(F32)
    out.append(x.astype(BF16))
    return out


def _dot_parts(a_parts, b_parts, order):
    acc = None
    for i, ai in enumerate(a_parts):
        for j, bj in enumerate(b_parts):
            if i + j < order:
                t = jnp.dot(ai, bj, preferred_element_type=F32)
                acc = t if acc is None else acc + t
    return acc


def _dot2(a, b):
    return _dot_parts(_split(a, 2), _split(b, 2), 2)


def _heads_mm(a, b, head0, wide):
    bs = _stack_heads(b, head0)
    return _dot2(a, bs) if wide else _dot(a, bs)


def _rwkv_scan_kernel(*refs, reverse, readout):
    if readout:
        (r_ref, k_ref, v_ref, a_ref, w_ref, kk_ref, ka_ref,
         yf_ref, af_ref, g_ref, rk_ref, lw_ref, lb_ref, o_ref, z_ref) = refs
    else:
        r_ref, k_ref, v_ref, a_ref, w_ref, kk_ref, ka_ref, o_ref, z_ref = refs
    tt = r_ref.shape[0]
    c = RWKV_CHUNK
    lanes = 2 * RWKV_N

    @pl.when(pl.program_id(2) == 0)
    def _():
        z_ref[...] = jnp.zeros_like(z_ref)

    lane = lax.broadcasted_iota(jnp.int32, (1, lanes), 1)
    head0 = lane < RWKV_N
    r2 = lax.broadcasted_iota(jnp.int32, (lanes, lanes), 0)
    c2 = lax.broadcasted_iota(jnp.int32, (lanes, lanes), 1)
    same_head = (r2 < RWKV_N) == (c2 < RWKV_N)
    eye2 = (r2 == c2).astype(F32)
    tr = lax.broadcasted_iota(jnp.int32, (c, c), 0)
    tc = lax.broadcasted_iota(jnp.int32, (c, c), 1)
    tri_b = ((tr <= tc) if reverse else (tr >= tc)).astype(BF16)
    pr = lax.broadcasted_iota(jnp.int32, (2 * c, 4 * c), 0)
    pc = lax.broadcasted_iota(jnp.int32, (2 * c, 4 * c), 1) % c
    incl = (pr >= c).astype(jnp.int32)
    pt = pr % c
    pmask = (pt - incl < pc) if reverse else (pt + incl > pc)
    eye_side = (lax.broadcasted_iota(jnp.int32, (c, 2 * c), 0)
                == lax.broadcasted_iota(jnp.int32, (c, 2 * c), 1) % c).astype(F32)
    kk_scale = kk_ref[...]
    ka_scale = ka_ref[...]
    lvl = (lax.broadcasted_iota(jnp.int32, (c, 2 * c), 0)
           ^ (lax.broadcasted_iota(jnp.int32, (c, 2 * c), 1) % c))

    n_chunks = tt // c
    n_pairs = r_ref.shape[1] // lanes
    order = list(range(n_chunks - 1, -1, -1) if reverse else range(n_chunks))
    sls = [(slice(ci * c, (ci + 1) * c), slice(p * lanes, (p + 1) * lanes))
           for p in range(n_pairs) for ci in order]
    vs = [v_ref[sl] for sl in sls]
    pre = []
    for sl in sls:
        k = k_ref[sl]
        asig = a_ref[sl]
        ld = w_ref[sl]
        kk = k * kk_scale[:, sl[1]]
        kk2 = kk * kk
        sq = jnp.where(head0,
                       jnp.sum(jnp.where(head0, kk2, 0.0), axis=-1, keepdims=True),
                       jnp.sum(jnp.where(head0, 0.0, kk2), axis=-1, keepdims=True))
        kk = kk * lax.rsqrt(jnp.maximum(sq, 1e-24))
        bvec = kk * asig
        kd = k * (1.0 + (asig - 1.0) * ka_scale[:, sl[1]])
        cum = _dot_parts([tri_b], _split(ld, 3), 3)
        tot = cum[0:1, :] if reverse else cum[c - 1:c, :]
        e_neg = jnp.exp(-cum)
        e_rem = jnp.exp(tot - cum)
        pre.append(dict(at=-kk * jnp.exp(cum - ld), rt=r_ref[sl] * jnp.exp(cum),
                        bt=bvec * e_neg, kt=kd * e_neg, bc=bvec * e_rem, kc=kd * e_rem,
                        wc=jnp.exp(tot)))
    ps = []
    for q in pre:
        lhs = jnp.concatenate([q["at"], q["rt"]], axis=0)
        rhs = jnp.concatenate([_stack_heads(q["bt"], head0), _stack_heads(q["kt"], head0)], axis=0)
        ps.append(jnp.where(pmask, _dot_nt(lhs, rhs), 0.0))
    a_ab = [p[:c, :2 * c] for p in ps]
    a_ak = [p[:c, 2 * c:] for p in ps]
    p_rb = [p[c:, :2 * c] for p in ps]
    p_rk = [p[c:, 2 * c:] for p in ps]
    base = [jnp.where((lvl >> 3) == 0, a, 0.0) for a in a_ab]
    pw2 = [_heads_mm(x, x, head0, False) for x in base]
    pw4 = [_heads_mm(x, x, head0, False) for x in pw2]
    inv = [eye_side + x for x in base]
    inv = [d + _heads_mm(d, x, head0, False) for d, x in zip(inv, pw2)]
    inv = [d + _heads_mm(d, x, head0, False) for d, x in zip(inv, pw4)]
    for bit in (3, 4, 5):
        off = [jnp.where((lvl >> bit) == 1, a, 0.0) for a in a_ab]
        ed = [_heads_mm(e, d, head0, False) for e, d in zip(off, inv)]
        inv = [d + _heads_mm(d, x, head0, False) for d, x in zip(inv, ed)]
    res = [eye_side - d + _heads_mm(a, d, head0, True) for a, d in zip(a_ab, inv)]
    inv = [d + _heads_mm(d, x, head0, False) for d, x in zip(inv, res)]
    at2 = [_heads_mm(d, q["at"], head0, True) for d, q in zip(inv, pre)]
    av = [_heads_mm(a, v, head0, False) for a, v in zip(a_ak, vs)]
    u_loc = [_heads_mm(d, x, head0, True) for d, x in zip(inv, av)]
    r_eff = [q["rt"] + _heads_mm(p, x, head0, False) for q, p, x in zip(pre, p_rb, at2)]
    y_loc = [_heads_mm(pb, u, head0, False) + _heads_mm(pk, v, head0, False)
             for pb, u, pk, v in zip(p_rb, u_loc, p_rk, vs)]
    m_t = [eye2 * q["wc"] + jnp.where(same_head, _dot_tn(q["bc"], x), 0.0)
           for q, x in zip(pre, at2)]
    n_t = [jnp.where(same_head, _dot_tn(q["bc"], u) + _dot_tn(q["kc"], v), 0.0)
           for q, u, v in zip(pre, u_loc, vs)]
    def head_sum(t):
        return jnp.where(head0,
                         jnp.sum(jnp.where(head0, t, 0.0), axis=-1, keepdims=True),
                         jnp.sum(jnp.where(head0, 0.0, t), axis=-1, keepdims=True))

    def read_out(sl, y):
        y = y + yf_ref[sl]
        yc = y - head_sum(y) * (1.0 / RWKV_N)
        var = head_sum(yc * yc) * (1.0 / RWKV_N)
        y = yc * lax.rsqrt(var + RWKV_GN_EPS) * lw_ref[:, sl[1]] + lb_ref[:, sl[1]]
        k = k_ref[sl]
        ka = ka_scale[:, sl[1]]
        k_both = k * (1.0 + (af_ref[sl] - 1.0) * ka) + k * (1.0 + (a_ref[sl] - 1.0) * ka)
        bonus = head_sum(r_ref[sl] * k_both * rk_ref[:, sl[1]]) * v_ref[sl]
        return (y + bonus) * g_ref[sl]

    for p in range(n_pairs):
        z = z_ref[p]
        for i in range(p * n_chunks, (p + 1) * n_chunks):
            y = _dot(r_eff[i], z) + y_loc[i]
            o_ref[sls[i]] = read_out(sls[i], y).astype(o_ref.dtype) if readout else y
            z = _dot2(m_t[i], z) + n_t[i]
        z_ref[p] = z


def _rwkv_scan(r, k, v, asig, logw, k_k, k_a, n_seq, reverse, readout=None):
    b, n_tot, d = r.shape
    lanes = 2 * RWKV_N
    tt = _pick(n_tot - n_seq, (256, 128, 64))
    assert n_seq % tt == 0 and d % lanes == 0
    n_lat, n_all = n_seq // tt, n_tot // tt
    n_pairs = _pick(d // lanes, (4, 2, 1))
    wide = n_pairs * lanes
    tile = pl.BlockSpec((None, tt, wide),
                        lambda i, h, j: (i, _scan_tile(j, n_lat, n_all, reverse), h))
    vec = pl.BlockSpec((1, wide), lambda i, h, j: (0, h))
    in_specs = [tile] * 5 + [vec, vec]
    args = [r, k, v, asig, logw, k_k.reshape(1, d), k_a.reshape(1, d)]
    if readout is not None:
        y_other, a_other, gate, r_k, ln_w, ln_b = readout
        in_specs += [tile] * 3 + [vec] * 3
        args += [y_other, a_other, gate, r_k.reshape(1, d), ln_w.reshape(1, d), ln_b.reshape(1, d)]
    return pl.pallas_call(
        functools.partial(_rwkv_scan_kernel, reverse=reverse, readout=readout is not None),
        grid=(b, d // wide, n_all),
        in_specs=in_specs,
        out_specs=tile,
        out_shape=jax.ShapeDtypeStruct((b, n_tot, d), F32 if readout is None else BF16),
        scratch_shapes=[pltpu.VMEM((n_pairs, lanes, lanes), F32)],
        compiler_params=_cparams("parallel", "parallel", "arbitrary"),
        name="rwkv_bwd" if reverse else "rwkv_fwd",
    )(*args)


def _pad_rows(w2, dr):
    zeros = jnp.zeros_like(w2[0])
    return jnp.concatenate([w2[0], zeros] if dr == 0 else [zeros, w2[1]], axis=0)


def _rwkv_mixer(us, n_seq, j, mu, w_rkv, w0, w1, w2, a0, a1, a2, g1, g2, k_k, k_a, r_k,
                ln_w, ln_b, w_out):
    b, n_tot, d = us.shape
    m = b * n_tot
    xr, xw, xk, xv, xa, xg = [t.reshape(m, d) for t in _rwkv_mix(us, mu, n_seq)]
    r = _mm(xr, w_rkv, widx=(j, 0))
    k = _mm(xk, w_rkv, widx=(j, 1))
    v = _mm(xv, w_rkv, widx=(j, 2))
    w_low = _mm(xw, jnp.concatenate([w1[0], w1[1]], axis=-1), act="tanh")
    a_low = _mm(xa, jnp.concatenate([a1[0], a1[1]], axis=-1))
    g_low = _mm(xg, g1, act="sigmoid")
    gate = _mm(g_low, g2)
    sh = (b, n_tot, d)
    r3, k3, v3 = r.reshape(sh), k.reshape(sh), v.reshape(sh)
    logw = [_mm(w_low, _pad_rows(w2, dr), w0[dr], act="rwkv_decay").reshape(sh) for dr in range(2)]
    asig = [_mm(a_low, _pad_rows(a2, dr), a0[dr], act="sigmoid").reshape(sh) for dr in range(2)]
    y_f = _rwkv_scan(r3, k3, v3, asig[0], logw[0], k_k, k_a, n_seq, False)
    hy = _rwkv_scan(r3, k3, v3, asig[1], logw[1], k_k, k_a, n_seq, True,
                    readout=(y_f, asig[0], gate.reshape(sh), r_k.reshape(-1), ln_w, ln_b))
    return _mm(hy.reshape(m, d), w_out, widx=(j,)).reshape(b, n_tot, d)


def kernel(x, c, ctx, c_ctx, ada_w, ada_b, norm_g, ffn_w_in, ffn_w_out, lru_w_in, lru_b_in, lru_conv_w, lru_conv_b, lru_gate_w, lru_gate_b, lru_log_lambda, lru_w_out, lru_b_out, gla_w_in, gla_b_r, gla_gate_w1, gla_gate_w2, gla_gate_b, gla_norm_g, gla_w_out, rwkv_mu, rwkv_w_rkv, rwkv_w0, rwkv_w1, rwkv_w2, rwkv_a0, rwkv_a1, rwkv_a2, rwkv_g1, rwkv_g2, rwkv_k_k, rwkv_k_a, rwkv_r_k, rwkv_ln_w, rwkv_ln_b, rwkv_w_out):
    b, n_seq, d = x.shape
    depth = ada_w.shape[0]
    assert b < 8
    cond = jnp.concatenate([c, c_ctx[None, :], jnp.zeros((8 - b - 1, d), F32)], axis=0)
    mods = _mods(cond, ada_w, ada_b).reshape(depth, 8, N_MOD, d)
    h_lat, h_ctx = x, ctx
    for i in range(depth):
        ctx_out = i < depth - 1
        kind, j = i % N_MIXERS, i // N_MIXERS
        col_major = i % 2 == 1
        m_lat, m_ctx = mods[i, :b], mods[i, b]
        us = _norm_mod(h_lat, h_ctx, norm_g[i, 0], m_lat, m_ctx, 0, 1, col_major,
                       F32 if kind == 2 else BF16)
        if kind == 0:
            y = _rglru_mixer(us, n_seq, j, lru_w_in, lru_b_in[j], lru_conv_w[j], lru_conv_b[j],
                             lru_gate_w[j], lru_gate_b[j], lru_log_lambda[j], lru_w_out,
                             lru_b_out[j])
        elif kind == 1:
            y = _gla_mixer(us, n_seq, j, gla_w_in, gla_b_r[j], gla_gate_w1[j], gla_gate_w2[j],
                           gla_gate_b[j], gla_norm_g[j], gla_w_out)
        else:
            y = _rwkv_mixer(us, n_seq, j, rwkv_mu[j], rwkv_w_rkv, rwkv_w0[j], rwkv_w1[j],
                            rwkv_w2[j], rwkv_a0[j], rwkv_a1[j], rwkv_a2[j], rwkv_g1[j],
                            rwkv_g2[j], rwkv_k_k[j], rwkv_k_a[j], rwkv_r_k[j], rwkv_ln_w[j],
                            rwkv_ln_b[j], rwkv_w_out)
        h_lat, h_ctx, us = _resid(h_lat, h_ctx if ctx_out else None, y, norm_g[i, 1], m_lat,
                                  m_ctx, 2, col_major, follow=(3, 4), g2=norm_g[i, 2])
        n_tot = us.shape[1]
        mid = _ffn_in(us.reshape(b * n_tot, d), ffn_w_in, i)
        y = _mm(mid, ffn_w_out, widx=(i,)).reshape(b, n_tot, d)
        h_lat, h_ctx, _ = _resid(h_lat, h_ctx, y, norm_g[i, 3], m_lat, m_ctx, 5, col_major)
    return h_lat
```

```python
import functools

import jax
import jax.numpy as jnp
from jax import lax
from jax.experimental import pallas as pl
from jax.experimental.pallas import tpu as pltpu

F32 = jnp.float32
BF16 = jnp.bfloat16

GRID_W = 64
N_MIXERS = 3
N_MOD = 6
NORM_EPS = 1e-6
LRU_BLOCKS = 16
LRU_C = 8.0
GLA_HEADS = 4
GLA_TAU = 16.0
GLA_CHUNK = 64
RWKV_N = 64
RWKV_CHUNK = 64
RWKV_GN_EPS = 64e-5
LANES = 128
SUBLANES = 8
VMEM_LIMIT = 52 * 1024 * 1024
PROJ_VMEM_BUDGET = 44 * 1024 * 1024


def _cparams(*sem):
    return pltpu.CompilerParams(dimension_semantics=sem, vmem_limit_bytes=VMEM_LIMIT)


def _pick(n, cands):
    for c in cands:
        if n % c == 0:
            return c
    return n


def _dot(a, b):
    return jnp.dot(a.astype(BF16), b.astype(BF16), preferred_element_type=F32)


def _dot_nt(a, b):
    return lax.dot_general(a.astype(BF16), b.astype(BF16), (((1,), (1,)), ((), ())),
                           preferred_element_type=F32)


def _dot_tn(a, b):
    return lax.dot_general(a.astype(BF16), b.astype(BF16), (((0,), (0,)), ((), ())),
                           preferred_element_type=F32)


def _sigmoid(x):
    return 1.0 / (1.0 + jnp.exp(-x))


def _softplus(x):
    return jnp.maximum(x, 0.0) + jnp.log(1.0 + jnp.exp(-jnp.abs(x)))


def _mods_kernel(x_ref, w_ref, b_ref, o_ref):
    x = x_ref[...]
    x = x * _sigmoid(x)
    o_ref[...] = _dot(x, w_ref[...]) + b_ref[...]


def _mods(cond, ada_w, ada_b):
    n_layers, d, n = ada_w.shape
    tn = _pick(n, (1024, 512, 256, 128))
    return pl.pallas_call(
        _mods_kernel,
        grid=(n_layers, n // tn),
        in_specs=[pl.BlockSpec((8, d), lambda l, j: (0, 0)),
                  pl.BlockSpec((None, d, tn), lambda l, j: (l, 0, j)),
                  pl.BlockSpec((None, 1, tn), lambda l, j: (l, 0, j))],
        out_specs=pl.BlockSpec((None, 8, tn), lambda l, j: (l, 0, j)),
        out_shape=jax.ShapeDtypeStruct((n_layers, 8, n), F32),
        compiler_params=_cparams("parallel", "parallel"),
        name="ada_mods",
    )(cond, ada_w, ada_b.reshape(n_layers, 1, n))


def _col_group(rows, n_ctx):
    return 2 if GRID_W % 2 == 0 and n_ctx % (2 * rows) == 0 else 1


def _norm_mod_kernel(*refs, k_shift, k_scale, n_lat, cols):
    def emit(h_ref, m_ref, o_ref, cols):
        r, d = h_ref.shape[0], g_ref.shape[1]
        for q in range(cols):
            x = h_ref[:, q * d:(q + 1) * d]
            y = x * lax.rsqrt(jnp.mean(x * x, axis=-1, keepdims=True) + NORM_EPS) * g_ref[...]
            y = y * (1.0 + m_ref[k_scale:k_scale + 1, :]) + m_ref[k_shift:k_shift + 1, :]
            o_ref[q * r:(q + 1) * r, :] = y.astype(o_ref.dtype)

    if n_lat is None:
        h_ref, g_ref, m_ref, o_ref = refs
        emit(h_ref, m_ref, o_ref, cols)
        return
    h_ref, c_ref, g_ref, m_ref, mc_ref, o_ref = refs
    is_ctx = pl.program_id(1) >= n_lat

    @pl.when(jnp.logical_not(is_ctx))
    def _():
        emit(h_ref, m_ref, o_ref, cols)

    @pl.when(is_ctx)
    def _():
        emit(c_ref, mc_ref, o_ref, 1)


def _norm_mod(h_lat, h_ctx, g, m_lat, m_ctx, k_shift, k_scale, col_major, out_dtype):
    b, n, d = h_lat.shape
    rows = n // GRID_W
    n_ctx = 0 if h_ctx is None else h_ctx.shape[1]
    g = g.reshape(1, d)
    cols = 1
    if col_major:
        cols = _col_group(rows, n_ctx)
        tt = cols * rows
        n_lat = GRID_W // cols
        h_in = h_lat.reshape(b, rows, GRID_W * d)
        h_spec = pl.BlockSpec((None, rows, cols * d),
                              lambda i, j: (i, 0, jnp.minimum(j, n_lat - 1)))
    else:
        tt = _pick(n_ctx if n_ctx else n, (256, 128, 64, 32, 16, 8))
        n_lat = n // tt
        h_in = h_lat
        h_spec = pl.BlockSpec((None, tt, d), lambda i, j: (i, jnp.minimum(j, n_lat - 1), 0))
    assert n_ctx % tt == 0 and n % tt == 0
    g_spec = pl.BlockSpec((1, d), lambda i, j: (0, 0))
    m_spec = pl.BlockSpec((None, N_MOD, d), lambda i, j: (i, 0, 0))
    if h_ctx is None:
        in_specs, args = [h_spec, g_spec, m_spec], (h_in, g, m_lat)
    else:
        in_specs = [h_spec,
                    pl.BlockSpec((None, tt, d), lambda i, j: (i, jnp.maximum(j - n_lat, 0), 0)),
                    g_spec, m_spec,
                    pl.BlockSpec((None, N_MOD, d), lambda i, j: (0, 0, 0))]
        args = (h_in, h_ctx, g, m_lat, m_ctx.reshape(1, N_MOD, d))
    return pl.pallas_call(
        functools.partial(_norm_mod_kernel, k_shift=k_shift, k_scale=k_scale,
                          n_lat=None if h_ctx is None else n_lat, cols=cols),
        grid=(b, n_lat + n_ctx // tt),
        in_specs=in_specs,
        out_specs=pl.BlockSpec((None, tt, d), lambda i, j: (i, j, 0)),
        out_shape=jax.ShapeDtypeStruct((b, n + n_ctx, d), out_dtype),
        compiler_params=_cparams("parallel", "arbitrary"),
        name="norm_mod",
    )(*args)


def _resid_kernel(*refs, k_gate, follow, n_lat, cols):
    it = iter(refs)
    both = n_lat is not None
    h_ref = next(it)
    c_ref = next(it) if both else None
    y_ref, g_ref = next(it), next(it)
    g2_ref = next(it) if follow else None
    m_ref = next(it)
    mc_ref = next(it) if both else None
    oh_ref = next(it)
    oc_ref = next(it) if both else None
    ou_ref = next(it) if follow else None

    def emit(src, mods, dst, cols):
        r, d = src.shape[0], g_ref.shape[1]
        for q in range(cols):
            rq, lq = slice(q * r, (q + 1) * r), slice(q * d, (q + 1) * d)
            y = y_ref[rq, :]
            yn = y * lax.rsqrt(jnp.mean(y * y, axis=-1, keepdims=True) + NORM_EPS) * g_ref[...]
            hn = src[:, lq] + mods[k_gate:k_gate + 1, :] * yn
            dst[:, lq] = hn
            if follow:
                k_shift, k_scale = follow
                u = hn * lax.rsqrt(jnp.mean(hn * hn, axis=-1, keepdims=True) + NORM_EPS) * g2_ref[...]
                u = u * (1.0 + mods[k_scale:k_scale + 1, :]) + mods[k_shift:k_shift + 1, :]
                ou_ref[rq, :] = u.astype(ou_ref.dtype)

    if not both:
        emit(h_ref, m_ref, oh_ref, cols)
        return
    is_ctx = pl.program_id(1) >= n_lat

    @pl.when(jnp.logical_not(is_ctx))
    def _():
        emit(h_ref, m_ref, oh_ref, cols)

    @pl.when(is_ctx)
    def _():
        emit(c_ref, mc_ref, oc_ref, 1)


def _resid(h_lat, h_ctx, y, g, m_lat, m_ctx, k_gate, col_major, follow=None, g2=None):
    b, n, d = h_lat.shape
    rows = n // GRID_W
    n_ctx = 0 if h_ctx is None else h_ctx.shape[1]
    both = h_ctx is not None
    cols = 1
    if col_major:
        cols = _col_group(rows, n_ctx)
        tt = cols * rows
        n_lat = GRID_W // cols
        h_in = h_lat.reshape(b, rows, GRID_W * d)
        h_spec = pl.BlockSpec((None, rows, cols * d),
                              lambda i, j: (i, 0, jnp.minimum(j, n_lat - 1)))
        lat_shape = jax.ShapeDtypeStruct((b, rows, GRID_W * d), F32)
    else:
        tt = _pick(n_ctx if n_ctx else n, (256, 128, 64, 32, 16, 8))
        n_lat = n // tt
        h_in = h_lat
        h_spec = pl.BlockSpec((None, tt, d), lambda i, j: (i, jnp.minimum(j, n_lat - 1), 0))
        lat_shape = jax.ShapeDtypeStruct((b, n, d), F32)
    assert n_ctx % tt == 0 and n % tt == 0
    c_spec = pl.BlockSpec((None, tt, d), lambda i, j: (i, jnp.maximum(j - n_lat, 0), 0))
    y_spec = pl.BlockSpec((None, tt, d), lambda i, j: (i, j, 0))
    g_spec = pl.BlockSpec((1, d), lambda i, j: (0, 0))
    in_specs, args = [h_spec], [h_in]
    if both:
        in_specs.append(c_spec)
        args.append(h_ctx)
    in_specs += [y_spec, g_spec]
    args += [y, g.reshape(1, d)]
    if follow:
        in_specs.append(g_spec)
        args.append(g2.reshape(1, d))
    in_specs.append(pl.BlockSpec((None, N_MOD, d), lambda i, j: (i, 0, 0)))
    args.append(m_lat)
    out_specs, out_shape = [h_spec], [lat_shape]
    if both:
        in_specs.append(pl.BlockSpec((None, N_MOD, d), lambda i, j: (0, 0, 0)))
        args.append(m_ctx.reshape(1, N_MOD, d))
        out_specs.append(c_spec)
        out_shape.append(jax.ShapeDtypeStruct((b, n_ctx, d), F32))
    if follow:
        out_specs.append(y_spec)
        out_shape.append(jax.ShapeDtypeStruct((b, n + n_ctx, d), BF16))
    outs = pl.pallas_call(
        functools.partial(_resid_kernel, k_gate=k_gate, follow=follow,
                          n_lat=n_lat if both else None, cols=cols),
        grid=(b, n_lat + n_ctx // tt),
        in_specs=in_specs,
        out_specs=out_specs,
        out_shape=out_shape,
        compiler_params=_cparams("parallel", "arbitrary"),
        name="resid",
    )(*args)
    outs = list(outs)
    new_lat = outs.pop(0).reshape(b, n, d)
    new_ctx = outs.pop(0) if both else None
    return new_lat, new_ctx, (outs.pop(0) if follow else None)


def _gelu_tanh(x):
    return 0.5 * x * (1.0 + jnp.tanh(0.7978845608028654 * (x + 0.044715 * x * x * x)))


def _apply_act(z, act):
    if act is None:
        return z
    if act == "gelu":
        return _gelu_tanh(z)
    if act == "tanh":
        return jnp.tanh(z)
    if act == "sigmoid":
        return _sigmoid(z)
    if act == "gla_gate":
        return -_softplus(-z) * (1.0 / GLA_TAU)
    if act == "rwkv_decay":
        return -jnp.exp(-_softplus(-z) - 0.5)
    raise ValueError(act)


def _proj_tiles(m, k, n, x_bytes, o_bytes, n_weights):
    best = None
    for tm in (2048, 1536, 1024, 512, 256, 128, 64, 32, 16, 8):
        if m % tm:
            continue
        for tn in (1024, 512, 256, 128, n):
            if n % tn:
                continue
            need = (2 * tm * k * x_bytes + n_weights * k * tn * (2 * 4 + 2)
                    + 2 * tm * tn * o_bytes + (tm * tn * 4 if o_bytes < 4 else 0))
            if need <= PROJ_VMEM_BUDGET and (best is None
                                             or (tm * tn, tn) > (best[0] * best[1], best[1])):
                best = (tm, tn)
    assert best is not None
    return best


def _row_chunks(tm, rows=512):
    rows = min(rows, tm)
    return [slice(r, r + rows) for r in range(0, tm, rows)]


def _mm_kernel(x_ref, w_ref, b_ref, o_ref, wb, *, act):
    @pl.when(pl.program_id(1) == 0)
    def _():
        wb[...] = w_ref[...].astype(BF16)

    for rows in _row_chunks(x_ref.shape[0]):
        z = _dot(x_ref[rows, :], wb[...]) + b_ref[...]
        o_ref[rows, :] = _apply_act(z, act).astype(o_ref.dtype)


def _mm(x, w, bias=None, act=None, out_dtype=F32, col0=0, ncols=None, widx=()):
    m, k = x.shape
    n_total = w.shape[-1]
    ncols = n_total if ncols is None else ncols
    tm, tn = _proj_tiles(m, k, ncols, x.dtype.itemsize, jnp.dtype(out_dtype).itemsize, 1)
    assert col0 % tn == 0 and w.ndim == 2 + len(widx)
    off = col0 // tn
    if bias is None:
        bias = jnp.zeros((ncols,), F32)
    return pl.pallas_call(
        functools.partial(_mm_kernel, act=act),
        grid=(ncols // tn, m // tm),
        in_specs=[pl.BlockSpec((tm, k), lambda j, i: (i, 0)),
                  pl.BlockSpec((None,) * len(widx) + (k, tn), lambda j, i: widx + (0, off + j)),
                  pl.BlockSpec((1, tn), lambda j, i: (0, j))],
        out_specs=pl.BlockSpec((tm, tn), lambda j, i: (i, j)),
        out_shape=jax.ShapeDtypeStruct((m, ncols), out_dtype),
        scratch_shapes=[pltpu.VMEM((k, tn), BF16)],
        compiler_params=_cparams("parallel", "arbitrary"),
        name="proj",
    )(x, w, bias.reshape(1, ncols).astype(F32))


def _ffn_in_kernel(x_ref, wg_ref, wu_ref, o_ref, wgb, wub):
    @pl.when(pl.program_id(1) == 0)
    def _():
        wgb[...] = wg_ref[...].astype(BF16)
        wub[...] = wu_ref[...].astype(BF16)

    for rows in _row_chunks(x_ref.shape[0]):
        x = x_ref[rows, :]
        gate = _dot(x, wgb[...])
        up = _dot(x, wub[...])
        o_ref[rows, :] = (gate * _sigmoid(gate) * up).astype(o_ref.dtype)


def _ffn_in(x, w_in, layer):
    m, k = x.shape
    d_ff = w_in.shape[-1] // 2
    tm, tn = _proj_tiles(m, k, d_ff, x.dtype.itemsize, 2, 2)
    off = d_ff // tn
    return pl.pallas_call(
        _ffn_in_kernel,
        grid=(d_ff // tn, m // tm),
        in_specs=[pl.BlockSpec((tm, k), lambda j, i: (i, 0)),
                  pl.BlockSpec((None, k, tn), lambda j, i: (layer, 0, j)),
                  pl.BlockSpec((None, k, tn), lambda j, i: (layer, 0, off + j))],
        out_specs=pl.BlockSpec((tm, tn), lambda j, i: (i, j)),
        out_shape=jax.ShapeDtypeStruct((m, d_ff), BF16),
        scratch_shapes=[pltpu.VMEM((k, tn), BF16)] * 2,
        compiler_params=_cparams("parallel", "arbitrary"),
        name="ffn_in",
    )(x, w_in, w_in)


def _scan_tile(step, n_lat, n_all, reverse):
    if reverse:
        return n_all - 1 - step
    n_ctx = n_all - n_lat
    return jnp.where(step < n_ctx, n_lat + step, step - n_ctx)


def _lru_conv_kernel(x_ref, p_ref, n_ref, w_ref, b_ref, o_ref, buf, *, n_lat, n_all):
    t = pl.program_id(1)
    tt = x_ref.shape[0]
    first = jnp.logical_or(t == 0, t == n_lat)
    last = jnp.logical_or(t == n_lat - 1, t == n_all - 1)
    buf[0:SUBLANES, :] = jnp.where(first, 0.0, p_ref[...])
    buf[SUBLANES:SUBLANES + tt, :] = x_ref[...]
    buf[SUBLANES + tt:2 * SUBLANES + tt, :] = jnp.where(last, 0.0, n_ref[...])
    acc = b_ref[...] + w_ref[0:1, :] * buf[SUBLANES - 2:SUBLANES - 2 + tt, :]
    acc = acc + w_ref[1:2, :] * buf[SUBLANES - 1:SUBLANES - 1 + tt, :]
    acc = acc + w_ref[2:3, :] * buf[SUBLANES:SUBLANES + tt, :]
    acc = acc + w_ref[3:4, :] * buf[SUBLANES + 1:SUBLANES + 1 + tt, :]
    o_ref[...] = acc


def _lru_conv(rec, conv_w, conv_b, n_seq):
    b, n_tot, w = rec.shape
    tt = _pick(n_tot - n_seq, (256, 128, 64, 32, 16, 8))
    assert n_seq % tt == 0 and conv_w.shape[0] == 4
    n_lat, n_all = n_seq // tt, n_tot // tt
    r8 = tt // SUBLANES
    last8 = n_tot // SUBLANES - 1
    return pl.pallas_call(
        functools.partial(_lru_conv_kernel, n_lat=n_lat, n_all=n_all),
        grid=(b, n_all),
        in_specs=[pl.BlockSpec((None, tt, w), lambda i, j: (i, j, 0)),
                  pl.BlockSpec((None, SUBLANES, w), lambda i, j: (i, jnp.maximum(j * r8 - 1, 0), 0)),
                  pl.BlockSpec((None, SUBLANES, w), lambda i, j: (i, jnp.minimum((j + 1) * r8, last8), 0)),
                  pl.BlockSpec((4, w), lambda i, j: (0, 0)),
                  pl.BlockSpec((1, w), lambda i, j: (0, 0))],
        out_specs=pl.BlockSpec((None, tt, w), lambda i, j: (i, j, 0)),
        out_shape=jax.ShapeDtypeStruct((b, n_tot, w), F32),
        scratch_shapes=[pltpu.VMEM((tt + 2 * SUBLANES, w), F32)],
        compiler_params=_cparams("parallel", "parallel"),
        name="lru_conv",
    )(rec, rec, rec, conv_w, conv_b.reshape(1, w))


def _lru_scan_kernel(*refs, reverse, combine, blk):
    if combine:
        x_ref, wc_ref, gb_ref, ll_ref, hf_ref, gate_ref, o_ref, a_s, b_s, carry = refs
    else:
        x_ref, wc_ref, gb_ref, ll_ref, o_ref, a_s, b_s, carry = refs
    tt, w = x_ref.shape
    nlb = w // LANES

    @pl.when(pl.program_id(1) == 0)
    def _():
        carry[...] = jnp.zeros_like(carry)

    def put(dst, n, val):
        if blk >= LANES:
            for p in range(blk // LANES):
                dst[n * (blk // LANES) + p] = val[:, p * LANES:(p + 1) * LANES]
        else:
            off = (n * blk) % LANES
            dst[(n * blk) // LANES, :, off:off + blk] = val

    sp = _softplus(-ll_ref[...])
    for n in range(w // blk):
        sl = slice(n * blk, (n + 1) * blk)
        xb = x_ref[:, sl]
        gates = _dot(xb, wc_ref[n])
        rr = _sigmoid(gates[:, :blk] + gb_ref[0:1, sl])
        ii = _sigmoid(gates[:, blk:] + gb_ref[1:2, sl])
        a = jnp.exp(-LRU_C * rr * sp[:, sl])
        put(a_s, n, a)
        om = 1.0 - a * a
        put(b_s, n, om * lax.rsqrt(jnp.maximum(om, 1e-30)) * (ii * xb))

    groups = tt // SUBLANES
    row = lax.broadcasted_iota(jnp.int32, (groups, SUBLANES, LANES), 1)
    for n in range(nlb):
        a = a_s[n].reshape(groups, SUBLANES, LANES)
        bt = b_s[n].reshape(groups, SUBLANES, LANES)
        for s in (1, 2, 4):
            keep = (row < SUBLANES - s) if reverse else (row >= s)
            shift = SUBLANES - s if reverse else s
            a_sh = jnp.where(keep, pltpu.roll(a, shift, 1), 1.0)
            b_sh = jnp.where(keep, pltpu.roll(bt, shift, 1), 0.0)
            bt = a * b_sh + bt
            a = a * a_sh
        c = carry[n]
        outs = [None] * groups
        for g in (range(groups - 1, -1, -1) if reverse else range(groups)):
            h = a[g] * c + bt[g]
            outs[g] = h
            c = h[0:1, :] if reverse else h[SUBLANES - 1:SUBLANES, :]
        carry[n] = c
        hl = jnp.concatenate(outs, axis=0)
        sl = slice(n * LANES, (n + 1) * LANES)
        if combine:
            o_ref[:, sl] = ((hl + hf_ref[:, sl]) * gate_ref[:, sl]).astype(o_ref.dtype)
        else:
            o_ref[:, sl] = hl


def _lru_scan(x, wcat, gate_b, log_lambda, n_seq, reverse, h_fwd=None, gate=None):
    b, n_tot, w = x.shape
    blk = w // LRU_BLOCKS
    tt = _pick(n_tot - n_seq, (256, 128, 64))
    assert n_seq % tt == 0 and tt % (SUBLANES * SUBLANES) == 0
    n_lat, n_all = n_seq // tt, n_tot // tt
    combine = h_fwd is not None
    tile = lambda i, j: (i, _scan_tile(j, n_lat, n_all, reverse), 0)
    in_specs = [pl.BlockSpec((None, tt, w), tile),
                pl.BlockSpec((LRU_BLOCKS, blk, 2 * blk), lambda i, j: (0, 0, 0)),
                pl.BlockSpec((2, w), lambda i, j: (0, 0)),
                pl.BlockSpec((1, w), lambda i, j: (0, 0))]
    args = [x, wcat, gate_b, log_lambda.reshape(1, w)]
    if combine:
        in_specs += [pl.BlockSpec((None, tt, w), tile), pl.BlockSpec((None, tt, w), tile)]
        args += [h_fwd, gate]
    return pl.pallas_call(
        functools.partial(_lru_scan_kernel, reverse=reverse, combine=combine, blk=blk),
        grid=(b, n_all),
        in_specs=in_specs,
        out_specs=pl.BlockSpec((None, tt, w), tile),
        out_shape=jax.ShapeDtypeStruct((b, n_tot, w), BF16 if combine else F32),
        scratch_shapes=([pltpu.VMEM((w // LANES, tt, LANES), F32)] * 2
                        + [pltpu.VMEM((w // LANES, 1, LANES), F32)]),
        compiler_params=_cparams("parallel", "arbitrary"),
        name="lru_scan_bwd" if reverse else "lru_scan_fwd",
    )(*args)


def _rglru_mixer(us, n_seq, j, w_in, b_in, conv_w, conv_b, gate_w, gate_b, log_lambda, w_out, b_out):
    b, n_tot, d = us.shape
    w = w_in.shape[-1] // 2
    x2 = us.reshape(b * n_tot, d)
    gate = _mm(x2, w_in, b_in[:w], act="gelu", col0=0, ncols=w, widx=(j,)).reshape(b, n_tot, w)
    rec = _mm(x2, w_in, b_in[w:], col0=w, ncols=w, widx=(j,)).reshape(b, n_tot, w)
    x = _lru_conv(rec, conv_w, conv_b, n_seq)
    wcat = jnp.concatenate([gate_w[:, 0], gate_w[:, 1]], axis=-1).astype(BF16)
    h_f = _lru_scan(x, wcat[0], gate_b[0], log_lambda[0], n_seq, False)
    hy = _lru_scan(x, wcat[1], gate_b[1], log_lambda[1], n_seq, True, h_fwd=h_f, gate=gate)
    y = _mm(hy.reshape(b * n_tot, w), w_out, b_out, widx=(j,))
    return y.reshape(b, n_tot, d)


def _gla_kernel(*refs, reverse, combine, scale):
    if combine:
        q_ref, k_ref, v_ref, g_ref, of_ref, r_ref, br_ref, ng_ref, o_ref, st = refs
    else:
        q_ref, k_ref, v_ref, g_ref, o_ref, st = refs
    tt = q_ref.shape[0]
    c = GLA_CHUNK

    @pl.when(pl.program_id(2) == 0)
    def _():
        st[...] = jnp.zeros_like(st)

    row = lax.broadcasted_iota(jnp.int32, (c, c), 0)
    col = lax.broadcasted_iota(jnp.int32, (c, c), 1)
    tri = (row <= col) if reverse else (row >= col)
    tri_b = tri.astype(BF16)
    n_chunks = tt // c
    n_heads, hv, hk = st.shape
    work = []
    for p in range(n_heads):
        lk = slice(p * hk, (p + 1) * hk)
        lv = slice(p * hv, (p + 1) * hv)
        for ci in (range(n_chunks - 1, -1, -1) if reverse else range(n_chunks)):
            sl = slice(ci * c, (ci + 1) * c)
            k = k_ref[sl, lk]
            v = v_ref[sl, lv].astype(BF16)
            bcum = _dot_parts([tri_b], _split(g_ref[sl, lk], 3), 3)
            b_last = bcum[0:1, :] if reverse else bcum[c - 1:c, :]
            q_e = (q_ref[sl, lk] * scale * jnp.exp(bcum)).astype(BF16)
            k_e = k * jnp.exp(-bcum)
            att = jnp.where(tri, _dot_nt(q_e, k_e), 0.0)
            o = _dot(att, v)
            if combine:
                o = o + of_ref[sl, lv]
            kv = _dot_tn(v, k * jnp.exp(b_last - bcum))
            work.append((sl, lv, q_e, o, jnp.exp(b_last), kv))
    for p in range(n_heads):
        s_t = st[p]
        for sl, lv, q_e, o, decay, kv in work[p * n_chunks:(p + 1) * n_chunks]:
            o = o + _dot_nt(q_e, s_t)
            if combine:
                o = o * lax.rsqrt(jnp.mean(o * o, axis=-1, keepdims=True) + NORM_EPS) * ng_ref[:, lv]
                r = r_ref[sl, lv] + br_ref[:, lv]
                o = o * (r * _sigmoid(r))
            o_ref[sl, lv] = o.astype(o_ref.dtype)
            s_t = s_t * decay + kv
        st[p] = s_t


def _gla_scan(qkvr, g, n_seq, reverse, o_fwd=None, b_r=None, norm_g=None):
    b, n_tot, _ = qkvr.shape
    dk = g.shape[-1]
    hk = dk // GLA_HEADS
    hv = 2 * hk
    tt = _pick(n_tot - n_seq, (256, 128, 64))
    assert n_seq % tt == 0
    n_lat, n_all = n_seq // tt, n_tot // tt
    combine = o_fwd is not None
    hps = 2
    wk, wv = hps * hk, hps * hv
    kq, kv = dk // wk, 2 * dk // wv
    tile = lambda i, h, j: _scan_tile(j, n_lat, n_all, reverse)
    in_specs = [pl.BlockSpec((None, tt, wk), lambda i, h, j: (i, tile(i, h, j), h)),
                pl.BlockSpec((None, tt, wk), lambda i, h, j: (i, tile(i, h, j), kq + h)),
                pl.BlockSpec((None, tt, wv), lambda i, h, j: (i, tile(i, h, j), kv + h)),
                pl.BlockSpec((None, tt, wk), lambda i, h, j: (i, tile(i, h, j), h))]
    args = [qkvr, qkvr, qkvr, g]
    o_spec = pl.BlockSpec((None, tt, wv), lambda i, h, j: (i, tile(i, h, j), h))
    if combine:
        kr = 4 * dk // wv
        vec = pl.BlockSpec((1, wv), lambda i, h, j: (0, h))
        in_specs += [o_spec, pl.BlockSpec((None, tt, wv), lambda i, h, j: (i, tile(i, h, j), kr + h)),
                     vec, vec]
        args += [o_fwd, qkvr, b_r.reshape(1, 2 * dk), norm_g.reshape(1, 2 * dk)]
    return pl.pallas_call(
        functools.partial(_gla_kernel, reverse=reverse, combine=combine, scale=float(hk) ** -0.5),
        grid=(b, GLA_HEADS // hps, n_all),
        in_specs=in_specs,
        out_specs=o_spec,
        out_shape=jax.ShapeDtypeStruct((b, n_tot, 2 * dk), BF16 if combine else F32),
        scratch_shapes=[pltpu.VMEM((hps, hv, hk), F32)],
        compiler_params=_cparams("parallel", "parallel", "arbitrary"),
        name="gla_bwd" if reverse else "gla_fwd",
    )(*args)


def _gla_mixer(us, n_seq, j, w_in, b_r, gate_w1, gate_w2, gate_b, norm_g, w_out):
    b, n_tot, d = us.shape
    m = b * n_tot
    x2 = us.reshape(m, d)
    dk = gate_w2.shape[-1]
    qkvr = _mm(x2, w_in, widx=(j,))
    low = _mm(x2, jnp.concatenate([gate_w1[0], gate_w1[1]], axis=-1))
    zeros = jnp.zeros_like(gate_w2[0])
    o = None
    for dr in range(2):
        w2 = jnp.concatenate([gate_w2[0], zeros] if dr == 0 else [zeros, gate_w2[1]], axis=0)
        g = _mm(low, w2, gate_b[dr], act="gla_gate")
        o = _gla_scan(qkvr.reshape(b, n_tot, -1), g.reshape(b, n_tot, dk), n_seq, dr == 1,
                      o_fwd=o, b_r=b_r, norm_g=norm_g)
    return _mm(o.reshape(m, 2 * dk), w_out, widx=(j,)).reshape(b, n_tot, d)


def _rwkv_mix_kernel(x_ref, p_ref, n_ref, mu_ref, *o_refs, n_lat, n_all):
    t = pl.program_id(1)
    tt = x_ref.shape[0]
    first = jnp.logical_or(t == 0, t == n_lat)
    last = jnp.logical_or(t == n_lat - 1, t == n_all - 1)
    x = x_ref[...]
    prev_row = jnp.where(first, 0.0, p_ref[SUBLANES - 1:SUBLANES, :])
    next_row = jnp.where(last, 0.0, n_ref[0:1, :])
    rid = lax.broadcasted_iota(jnp.int32, x.shape, 0)
    up = jnp.where(rid == 0, prev_row, pltpu.roll(x, 1, 0))
    dn = jnp.where(rid == tt - 1, next_row, pltpu.roll(x, tt - 1, 0))
    dx = 0.5 * (up + dn) - x
    for i, o_ref in enumerate(o_refs):
        o_ref[...] = (x + dx * mu_ref[i:i + 1, :]).astype(o_ref.dtype)


def _rwkv_mix(us, mu, n_seq):
    b, n_tot, d = us.shape
    n_mix = mu.shape[0]
    tt = _pick(n_tot - n_seq, (256, 128, 64, 32, 16, 8))
    assert n_seq % tt == 0
    n_lat, n_all = n_seq // tt, n_tot // tt
    r8 = tt // SUBLANES
    last8 = n_tot // SUBLANES - 1
    out = jax.ShapeDtypeStruct((b, n_tot, d), BF16)
    return pl.pallas_call(
        functools.partial(_rwkv_mix_kernel, n_lat=n_lat, n_all=n_all),
        grid=(b, n_all),
        in_specs=[pl.BlockSpec((None, tt, d), lambda i, j: (i, j, 0)),
                  pl.BlockSpec((None, SUBLANES, d), lambda i, j: (i, jnp.maximum(j * r8 - 1, 0), 0)),
                  pl.BlockSpec((None, SUBLANES, d), lambda i, j: (i, jnp.minimum((j + 1) * r8, last8), 0)),
                  pl.BlockSpec((n_mix, d), lambda i, j: (0, 0))],
        out_specs=[pl.BlockSpec((None, tt, d), lambda i, j: (i, j, 0))] * n_mix,
        out_shape=[out] * n_mix,
        compiler_params=_cparams("parallel", "parallel"),
        name="rwkv_mix",
    )(us, us, us, mu)


def _stack_heads(x, head0):
    return jnp.concatenate([jnp.where(head0, x, 0.0), jnp.where(head0, 0.0, x)], axis=0)


def _split(x, parts):
    out = []
    for _ in range(parts - 1):
        hi = x.astype(BF16)
        out.append(hi)
        x = x - hi.astype(F32)
    out.append(x.astype(BF16))
    return out


def _dot_parts(a_parts, b_parts, order):
    acc = None
    for i, ai in enumerate(a_parts):
        for j, bj in enumerate(b_parts):
            if i + j < order:
                t = jnp.dot(ai, bj, preferred_element_type=F32)
                acc = t if acc is None else acc + t
    return acc


def _dot2(a, b):
    return _dot_parts(_split(a, 2), _split(b, 2), 2)


def _heads_mm(a, b, head0, wide):
    bs = _stack_heads(b, head0)
    return _dot2(a, bs) if wide else _dot(a, bs)


def _rwkv_scan_kernel(*refs, reverse, readout):
    if readout:
        (r_ref, k_ref, v_ref, a_ref, w_ref, kk_ref, ka_ref,
         yf_ref, af_ref, g_ref, rk_ref, lw_ref, lb_ref, o_ref, z_ref) = refs
    else:
        r_ref, k_ref, v_ref, a_ref, w_ref, kk_ref, ka_ref, o_ref, z_ref = refs
    tt = r_ref.shape[0]
    c = RWKV_CHUNK
    lanes = 2 * RWKV_N

    @pl.when(pl.program_id(2) == 0)
    def _():
        z_ref[...] = jnp.zeros_like(z_ref)

    lane = lax.broadcasted_iota(jnp.int32, (1, lanes), 1)
    head0 = lane < RWKV_N
    r2 = lax.broadcasted_iota(jnp.int32, (lanes, lanes), 0)
    c2 = lax.broadcasted_iota(jnp.int32, (lanes, lanes), 1)
    same_head = (r2 < RWKV_N) == (c2 < RWKV_N)
    eye2 = (r2 == c2).astype(F32)
    tr = lax.broadcasted_iota(jnp.int32, (c, c), 0)
    tc = lax.broadcasted_iota(jnp.int32, (c, c), 1)
    tri_b = ((tr <= tc) if reverse else (tr >= tc)).astype(BF16)
    pr = lax.broadcasted_iota(jnp.int32, (2 * c, 4 * c), 0)
    pc = lax.broadcasted_iota(jnp.int32, (2 * c, 4 * c), 1) % c
    incl = (pr >= c).astype(jnp.int32)
    pt = pr % c
    pmask = (pt - incl < pc) if reverse else (pt + incl > pc)
    eye_side = (lax.broadcasted_iota(jnp.int32, (c, 2 * c), 0)
                == lax.broadcasted_iota(jnp.int32, (c, 2 * c), 1) % c).astype(F32)
    kk_scale = kk_ref[...]
    ka_scale = ka_ref[...]
    lvl = (lax.broadcasted_iota(jnp.int32, (c, 2 * c), 0)
           ^ (lax.broadcasted_iota(jnp.int32, (c, 2 * c), 1) % c))

    n_chunks = tt // c
    n_pairs = r_ref.shape[1] // lanes
    order = list(range(n_chunks - 1, -1, -1) if reverse else range(n_chunks))
    sls = [(slice(ci * c, (ci + 1) * c), slice(p * lanes, (p + 1) * lanes))
           for p in range(n_pairs) for ci in order]
    vs = [v_ref[sl] for sl in sls]
    pre = []
    for sl in sls:
        k = k_ref[sl]
        asig = a_ref[sl]
        ld = w_ref[sl]
        kk = k * kk_scale[:, sl[1]]
        kk2 = kk * kk
        sq = jnp.where(head0,
                       jnp.sum(jnp.where(head0, kk2, 0.0), axis=-1, keepdims=True),
                       jnp.sum(jnp.where(head0, 0.0, kk2), axis=-1, keepdims=True))
        kk = kk * lax.rsqrt(jnp.maximum(sq, 1e-24))
        bvec = kk * asig
        kd = k * (1.0 + (asig - 1.0) * ka_scale[:, sl[1]])
        cum = _dot_parts([tri_b], _split(ld, 2), 2)
        tot = cum[0:1, :] if reverse else cum[c - 1:c, :]
        e_neg = jnp.exp(-cum)
        e_rem = jnp.exp(tot - cum)
        pre.append(dict(at=-kk * jnp.exp(cum - ld), rt=r_ref[sl] * jnp.exp(cum),
                        bt=bvec * e_neg, kt=kd * e_neg, bc=bvec * e_rem, kc=kd * e_rem,
                        wc=jnp.exp(tot)))
    ps = []
    for q in pre:
        lhs = jnp.concatenate([q["at"], q["rt"]], axis=0)
        rhs = jnp.concatenate([_stack_heads(q["bt"], head0), _stack_heads(q["kt"], head0)], axis=0)
        ps.append(jnp.where(pmask, _dot_nt(lhs, rhs), 0.0))
    a_ab = [p[:c, :2 * c] for p in ps]
    a_ak = [p[:c, 2 * c:] for p in ps]
    p_rb = [p[c:, :2 * c] for p in ps]
    p_rk = [p[c:, 2 * c:] for p in ps]
    base = [jnp.where((lvl >> 3) == 0, a, 0.0) for a in a_ab]
    pw2 = [_heads_mm(x, x, head0, False) for x in base]
    pw4 = [_heads_mm(x, x, head0, False) for x in pw2]
    inv = [eye_side + x for x in base]
    inv = [d + _heads_mm(d, x, head0, False) for d, x in zip(inv, pw2)]
    inv = [d + _heads_mm(d, x, head0, False) for d, x in zip(inv, pw4)]
    for bit in (3, 4, 5):
        off = [jnp.where((lvl >> bit) == 1, a, 0.0) for a in a_ab]
        ed = [_heads_mm(e, d, head0, False) for e, d in zip(off, inv)]
        inv = [d + _heads_mm(d, x, head0, False) for d, x in zip(inv, ed)]
    res = [eye_side - d + _heads_mm(a, d, head0, True) for a, d in zip(a_ab, inv)]
    inv = [d + _heads_mm(d, x, head0, False) for d, x in zip(inv, res)]
    at2 = [_heads_mm(d, q["at"], head0, True) for d, q in zip(inv, pre)]
    av = [_heads_mm(a, v, head0, False) for a, v in zip(a_ak, vs)]
    u_loc = [_heads_mm(d, x, head0, True) for d, x in zip(inv, av)]
    r_eff = [q["rt"] + _heads_mm(p, x, head0, False) for q, p, x in zip(pre, p_rb, at2)]
    y_loc = [_heads_mm(pb, u, head0, False) + _heads_mm(pk, v, head0, False)
             for pb, u, pk, v in zip(p_rb, u_loc, p_rk, vs)]
    m_t = [eye2 * q["wc"] + jnp.where(same_head, _dot_tn(q["bc"], x), 0.0)
           for q, x in zip(pre, at2)]
    n_t = [jnp.where(same_head, _dot_tn(q["bc"], u) + _dot_tn(q["kc"], v), 0.0)
           for q, u, v in zip(pre, u_loc, vs)]
    def head_sum(t):
        return jnp.where(head0,
                         jnp.sum(jnp.where(head0, t, 0.0), axis=-1, keepdims=True),
                         jnp.sum(jnp.where(head0, 0.0, t), axis=-1, keepdims=True))

    def read_out(sl, y):
        y = y + yf_ref[sl]
        yc = y - head_sum(y) * (1.0 / RWKV_N)
        var = head_sum(yc * yc) * (1.0 / RWKV_N)
        y = yc * lax.rsqrt(var + RWKV_GN_EPS) * lw_ref[:, sl[1]] + lb_ref[:, sl[1]]
        k = k_ref[sl]
        ka = ka_scale[:, sl[1]]
        k_both = k * (1.0 + (af_ref[sl] - 1.0) * ka) + k * (1.0 + (a_ref[sl] - 1.0) * ka)
        bonus = head_sum(r_ref[sl] * k_both * rk_ref[:, sl[1]]) * v_ref[sl]
        return (y + bonus) * g_ref[sl]

    for p in range(n_pairs):
        z = z_ref[p]
        for i in range(p * n_chunks, (p + 1) * n_chunks):
            y = _dot(r_eff[i], z) + y_loc[i]
            o_ref[sls[i]] = read_out(sls[i], y).astype(o_ref.dtype) if readout else y
            z = _dot2(m_t[i], z) + n_t[i]
        z_ref[p] = z


def _rwkv_scan(r, k, v, asig, logw, k_k, k_a, n_seq, reverse, readout=None):
    b, n_tot, d = r.shape
    lanes = 2 * RWKV_N
    tt = _pick(n_tot - n_seq, (256, 128, 64))
    assert n_seq % tt == 0 and d % lanes == 0
    n_lat, n_all = n_seq // tt, n_tot // tt
    n_pairs = _pick(d // lanes, (4, 2, 1))
    wide = n_pairs * lanes
    tile = pl.BlockSpec((None, tt, wide),
                        lambda i, h, j: (i, _scan_tile(j, n_lat, n_all, reverse), h))
    vec = pl.BlockSpec((1, wide), lambda i, h, j: (0, h))
    in_specs = [tile] * 5 + [vec, vec]
    args = [r, k, v, asig, logw, k_k.reshape(1, d), k_a.reshape(1, d)]
    if readout is not None:
        y_other, a_other, gate, r_k, ln_w, ln_b = readout
        in_specs += [tile] * 3 + [vec] * 3
        args += [y_other, a_other, gate, r_k.reshape(1, d), ln_w.reshape(1, d), ln_b.reshape(1, d)]
    return pl.pallas_call(
        functools.partial(_rwkv_scan_kernel, reverse=reverse, readout=readout is not None),
        grid=(b, d // wide, n_all),
        in_specs=in_specs,
        out_specs=tile,
        out_shape=jax.ShapeDtypeStruct((b, n_tot, d), F32 if readout is None else BF16),
        scratch_shapes=[pltpu.VMEM((n_pairs, lanes, lanes), F32)],
        compiler_params=_cparams("parallel", "parallel", "arbitrary"),
        name="rwkv_bwd" if reverse else "rwkv_fwd",
    )(*args)


def _pad_rows(w2, dr):
    zeros = jnp.zeros_like(w2[0])
    return jnp.concatenate([w2[0], zeros] if dr == 0 else [zeros, w2[1]], axis=0)


def _rwkv_mixer(us, n_seq, j, mu, w_rkv, w0, w1, w2, a0, a1, a2, g1, g2, k_k, k_a, r_k,
                ln_w, ln_b, w_out):
    b, n_tot, d = us.shape
    m = b * n_tot
    xr, xw, xk, xv, xa, xg = [t.reshape(m, d) for t in _rwkv_mix(us, mu, n_seq)]
    r = _mm(xr, w_rkv, widx=(j, 0))
    k = _mm(xk, w_rkv, widx=(j, 1))
    v = _mm(xv, w_rkv, widx=(j, 2))
    w_low = _mm(xw, jnp.concatenate([w1[0], w1[1]], axis=-1), act="tanh")
    a_low = _mm(xa, jnp.concatenate([a1[0], a1[1]], axis=-1))
    g_low = _mm(xg, g1, act="sigmoid")
    gate = _mm(g_low, g2)
    sh = (b, n_tot, d)
    r3, k3, v3 = r.reshape(sh), k.reshape(sh), v.reshape(sh)
    logw = [_mm(w_low, _pad_rows(w2, dr), w0[dr], act="rwkv_decay").reshape(sh) for dr in range(2)]
    asig = [_mm(a_low, _pad_rows(a2, dr), a0[dr], act="sigmoid").reshape(sh) for dr in range(2)]
    y_f = _rwkv_scan(r3, k3, v3, asig[0], logw[0], k_k, k_a, n_seq, False)
    hy = _rwkv_scan(r3, k3, v3, asig[1], logw[1], k_k, k_a, n_seq, True,
                    readout=(y_f, asig[0], gate.reshape(sh), r_k.reshape(-1), ln_w, ln_b))
    return _mm(hy.reshape(m, d), w_out, widx=(j,)).reshape(b, n_tot, d)


def kernel(x, c, ctx, c_ctx, ada_w, ada_b, norm_g, ffn_w_in, ffn_w_out, lru_w_in, lru_b_in, lru_conv_w, lru_conv_b, lru_gate_w, lru_gate_b, lru_log_lambda, lru_w_out, lru_b_out, gla_w_in, gla_b_r, gla_gate_w1, gla_gate_w2, gla_gate_b, gla_norm_g, gla_w_out, rwkv_mu, rwkv_w_rkv, rwkv_w0, rwkv_w1, rwkv_w2, rwkv_a0, rwkv_a1, rwkv_a2, rwkv_g1, rwkv_g2, rwkv_k_k, rwkv_k_a, rwkv_r_k, rwkv_ln_w, rwkv_ln_b, rwkv_w_out):
    b, n_seq, d = x.shape
    depth = ada_w.shape[0]
    assert b < 8
    cond = jnp.concatenate([c, c_ctx[None, :], jnp.zeros((8 - b - 1, d), F32)], axis=0)
    mods = _mods(cond, ada_w, ada_b).reshape(depth, 8, N_MOD, d)
    h_lat, h_ctx = x, ctx
    for i in range(depth):
        ctx_out = i < depth - 1
        kind, j = i % N_MIXERS, i // N_MIXERS
        col_major = i % 2 == 1
        m_lat, m_ctx = mods[i, :b], mods[i, b]
        us = _norm_mod(h_lat, h_ctx, norm_g[i, 0], m_lat, m_ctx, 0, 1, col_major,
                       F32 if kind == 2 else BF16)
        if kind == 0:
            y = _rglru_mixer(us, n_seq, j, lru_w_in, lru_b_in[j], lru_conv_w[j], lru_conv_b[j],
                             lru_gate_w[j], lru_gate_b[j], lru_log_lambda[j], lru_w_out,
                             lru_b_out[j])
        elif kind == 1:
            y = _gla_mixer(us, n_seq, j, gla_w_in, gla_b_r[j], gla_gate_w1[j], gla_gate_w2[j],
                           gla_gate_b[j], gla_norm_g[j], gla_w_out)
        else:
            y = _rwkv_mixer(us, n_seq, j, rwkv_mu[j], rwkv_w_rkv, rwkv_w0[j], rwkv_w1[j],
                            rwkv_w2[j], rwkv_a0[j], rwkv_a1[j], rwkv_a2[j], rwkv_g1[j],
                            rwkv_g2[j], rwkv_k_k[j], rwkv_k_a[j], rwkv_r_k[j], rwkv_ln_w[j],
                            rwkv_ln_b[j], rwkv_w_out)
        h_lat, h_ctx, us = _resid(h_lat, h_ctx if ctx_out else None, y, norm_g[i, 1], m_lat,
                                  m_ctx, 2, col_major, follow=(3, 4), g2=norm_g[i, 2])
        n_tot = us.shape[1]
        mid = _ffn_in(us.reshape(b * n_tot, d), ffn_w_in, i)
        y = _mm(mid, ffn_w_out, widx=(i,)).reshape(b, n_tot, d)
        h_lat, h_ctx, _ = _resid(h_lat, h_ctx, y, norm_g[i, 3], m_lat, m_ctx, 5, col_major)
    return h_lat
```

```python
import functools

import jax
import jax.numpy as jnp
from jax import lax
from jax.experimental import pallas as pl
from jax.experimental.pallas import tpu as pltpu

F32 = jnp.float32
BF16 = jnp.bfloat16

GRID_W = 64
N_MIXERS = 3
N_MOD = 6
NORM_EPS = 1e-6
LRU_BLOCKS = 16
LRU_C = 8.0
GLA_HEADS = 4
GLA_TAU = 16.0
GLA_CHUNK = 64
RWKV_N = 64
RWKV_CHUNK = 64
RWKV_GN_EPS = 64e-5
LANES = 128
SUBLANES = 8
VMEM_LIMIT = 56 * 1024 * 1024
PROJ_VMEM_BUDGET = 47 * 1024 * 1024


def _cparams(*sem):
    return pltpu.CompilerParams(dimension_semantics=sem, vmem_limit_bytes=VMEM_LIMIT)


def _pick(n, cands):
    for c in cands:
        if n % c == 0:
            return c
    return n


def _dot(a, b):
    return jnp.dot(a.astype(BF16), b.astype(BF16), preferred_element_type=F32)


def _dot_nt(a, b):
    return lax.dot_general(a.astype(BF16), b.astype(BF16), (((1,), (1,)), ((), ())),
                           preferred_element_type=F32)


def _dot_tn(a, b):
    return lax.dot_general(a.astype(BF16), b.astype(BF16), (((0,), (0,)), ((), ())),
                           preferred_element_type=F32)


def _sigmoid(x):
    return 1.0 / (1.0 + jnp.exp(-x))


def _softplus(x):
    return jnp.maximum(x, 0.0) + jnp.log(1.0 + jnp.exp(-jnp.abs(x)))


def _mods_kernel(x_ref, w_ref, b_ref, o_ref):
    x = x_ref[...]
    x = x * _sigmoid(x)
    o_ref[...] = _dot(x, w_ref[...]) + b_ref[...]


def _mods(cond, ada_w, ada_b):
    n_layers, d, n = ada_w.shape
    tn = _pick(n, (1024, 512, 256, 128))
    return pl.pallas_call(
        _mods_kernel,
        grid=(n_layers, n // tn),
        in_specs=[pl.BlockSpec((8, d), lambda l, j: (0, 0)),
                  pl.BlockSpec((None, d, tn), lambda l, j: (l, 0, j)),
                  pl.BlockSpec((None, 1, tn), lambda l, j: (l, 0, j))],
        out_specs=pl.BlockSpec((None, 8, tn), lambda l, j: (l, 0, j)),
        out_shape=jax.ShapeDtypeStruct((n_layers, 8, n), F32),
        compiler_params=_cparams("parallel", "parallel"),
        name="ada_mods",
    )(cond, ada_w, ada_b.reshape(n_layers, 1, n))


def _col_group(rows, n_ctx):
    return 2 if GRID_W % 2 == 0 and n_ctx % (2 * rows) == 0 else 1


def _norm_mod_kernel(*refs, k_shift, k_scale, n_lat, cols):
    def emit(h_ref, m_ref, o_ref, cols):
        r, d = h_ref.shape[0], g_ref.shape[1]
        for q in range(cols):
            x = h_ref[:, q * d:(q + 1) * d]
            y = x * lax.rsqrt(jnp.mean(x * x, axis=-1, keepdims=True) + NORM_EPS) * g_ref[...]
            y = y * (1.0 + m_ref[k_scale:k_scale + 1, :]) + m_ref[k_shift:k_shift + 1, :]
            o_ref[q * r:(q + 1) * r, :] = y.astype(o_ref.dtype)

    if n_lat is None:
        h_ref, g_ref, m_ref, o_ref = refs
        emit(h_ref, m_ref, o_ref, cols)
        return
    h_ref, c_ref, g_ref, m_ref, mc_ref, o_ref = refs
    is_ctx = pl.program_id(1) >= n_lat

    @pl.when(jnp.logical_not(is_ctx))
    def _():
        emit(h_ref, m_ref, o_ref, cols)

    @pl.when(is_ctx)
    def _():
        emit(c_ref, mc_ref, o_ref, 1)


def _norm_mod(h_lat, h_ctx, g, m_lat, m_ctx, k_shift, k_scale, col_major, out_dtype):
    b, n, d = h_lat.shape
    rows = n // GRID_W
    n_ctx = 0 if h_ctx is None else h_ctx.shape[1]
    g = g.reshape(1, d)
    cols = 1
    if col_major:
        cols = _col_group(rows, n_ctx)
        tt = cols * rows
        n_lat = GRID_W // cols
        h_in = h_lat.reshape(b, rows, GRID_W * d)
        h_spec = pl.BlockSpec((None, rows, cols * d),
                              lambda i, j: (i, 0, jnp.minimum(j, n_lat - 1)))
    else:
        tt = _pick(n_ctx if n_ctx else n, (256, 128, 64, 32, 16, 8))
        n_lat = n // tt
        h_in = h_lat
        h_spec = pl.BlockSpec((None, tt, d), lambda i, j: (i, jnp.minimum(j, n_lat - 1), 0))
    assert n_ctx % tt == 0 and n % tt == 0
    g_spec = pl.BlockSpec((1, d), lambda i, j: (0, 0))
    m_spec = pl.BlockSpec((None, N_MOD, d), lambda i, j: (i, 0, 0))
    if h_ctx is None:
        in_specs, args = [h_spec, g_spec, m_spec], (h_in, g, m_lat)
    else:
        in_specs = [h_spec,
                    pl.BlockSpec((None, tt, d), lambda i, j: (i, jnp.maximum(j - n_lat, 0), 0)),
                    g_spec, m_spec,
                    pl.BlockSpec((None, N_MOD, d), lambda i, j: (0, 0, 0))]
        args = (h_in, h_ctx, g, m_lat, m_ctx.reshape(1, N_MOD, d))
    return pl.pallas_call(
        functools.partial(_norm_mod_kernel, k_shift=k_shift, k_scale=k_scale,
                          n_lat=None if h_ctx is None else n_lat, cols=cols),
        grid=(b, n_lat + n_ctx // tt),
        in_specs=in_specs,
        out_specs=pl.BlockSpec((None, tt, d), lambda i, j: (i, j, 0)),
        out_shape=jax.ShapeDtypeStruct((b, n + n_ctx, d), out_dtype),
        compiler_params=_cparams("parallel", "arbitrary"),
        name="norm_mod",
    )(*args)


def _resid_kernel(*refs, k_gate, follow, n_lat, cols):
    it = iter(refs)
    both = n_lat is not None
    h_ref = next(it)
    c_ref = next(it) if both else None
    y_ref, g_ref = next(it), next(it)
    g2_ref = next(it) if follow else None
    m_ref = next(it)
    mc_ref = next(it) if both else None
    oh_ref = next(it)
    oc_ref = next(it) if both else None
    ou_ref = next(it) if follow else None

    def emit(src, mods, dst, cols):
        r, d = src.shape[0], g_ref.shape[1]
        for q in range(cols):
            rq, lq = slice(q * r, (q + 1) * r), slice(q * d, (q + 1) * d)
            y = y_ref[rq, :]
            yn = y * lax.rsqrt(jnp.mean(y * y, axis=-1, keepdims=True) + NORM_EPS) * g_ref[...]
            hn = src[:, lq] + mods[k_gate:k_gate + 1, :] * yn
            dst[:, lq] = hn
            if follow:
                k_shift, k_scale = follow
                u = hn * lax.rsqrt(jnp.mean(hn * hn, axis=-1, keepdims=True) + NORM_EPS) * g2_ref[...]
                u = u * (1.0 + mods[k_scale:k_scale + 1, :]) + mods[k_shift:k_shift + 1, :]
                ou_ref[rq, :] = u.astype(ou_ref.dtype)

    if not both:
        emit(h_ref, m_ref, oh_ref, cols)
        return
    is_ctx = pl.program_id(1) >= n_lat

    @pl.when(jnp.logical_not(is_ctx))
    def _():
        emit(h_ref, m_ref, oh_ref, cols)

    @pl.when(is_ctx)
    def _():
        emit(c_ref, mc_ref, oc_ref, 1)


def _resid(h_lat, h_ctx, y, g, m_lat, m_ctx, k_gate, col_major, follow=None, g2=None):
    b, n, d = h_lat.shape
    rows = n // GRID_W
    n_ctx = 0 if h_ctx is None else h_ctx.shape[1]
    both = h_ctx is not None
    cols = 1
    if col_major:
        cols = _col_group(rows, n_ctx)
        tt = cols * rows
        n_lat = GRID_W // cols
        h_in = h_lat.reshape(b, rows, GRID_W * d)
        h_spec = pl.BlockSpec((None, rows, cols * d),
                              lambda i, j: (i, 0, jnp.minimum(j, n_lat - 1)))
        lat_shape = jax.ShapeDtypeStruct((b, rows, GRID_W * d), F32)
    else:
        tt = _pick(n_ctx if n_ctx else n, (256, 128, 64, 32, 16, 8))
        n_lat = n // tt
        h_in = h_lat
        h_spec = pl.BlockSpec((None, tt, d), lambda i, j: (i, jnp.minimum(j, n_lat - 1), 0))
        lat_shape = jax.ShapeDtypeStruct((b, n, d), F32)
    assert n_ctx % tt == 0 and n % tt == 0
    c_spec = pl.BlockSpec((None, tt, d), lambda i, j: (i, jnp.maximum(j - n_lat, 0), 0))
    y_spec = pl.BlockSpec((None, tt, d), lambda i, j: (i, j, 0))
    g_spec = pl.BlockSpec((1, d), lambda i, j: (0, 0))
    in_specs, args = [h_spec], [h_in]
    if both:
        in_specs.append(c_spec)
        args.append(h_ctx)
    in_specs += [y_spec, g_spec]
    args += [y, g.reshape(1, d)]
    if follow:
        in_specs.append(g_spec)
        args.append(g2.reshape(1, d))
    in_specs.append(pl.BlockSpec((None, N_MOD, d), lambda i, j: (i, 0, 0)))
    args.append(m_lat)
    out_specs, out_shape = [h_spec], [lat_shape]
    if both:
        in_specs.append(pl.BlockSpec((None, N_MOD, d), lambda i, j: (0, 0, 0)))
        args.append(m_ctx.reshape(1, N_MOD, d))
        out_specs.append(c_spec)
        out_shape.append(jax.ShapeDtypeStruct((b, n_ctx, d), F32))
    if follow:
        out_specs.append(y_spec)
        out_shape.append(jax.ShapeDtypeStruct((b, n + n_ctx, d), BF16))
    outs = pl.pallas_call(
        functools.partial(_resid_kernel, k_gate=k_gate, follow=follow,
                          n_lat=n_lat if both else None, cols=cols),
        grid=(b, n_lat + n_ctx // tt),
        in_specs=in_specs,
        out_specs=out_specs,
        out_shape=out_shape,
        compiler_params=_cparams("parallel", "arbitrary"),
        name="resid",
    )(*args)
    outs = list(outs)
    new_lat = outs.pop(0).reshape(b, n, d)
    new_ctx = outs.pop(0) if both else None
    return new_lat, new_ctx, (outs.pop(0) if follow else None)


def _gelu_tanh(x):
    return 0.5 * x * (1.0 + jnp.tanh(0.7978845608028654 * (x + 0.044715 * x * x * x)))


def _apply_act(z, act):
    if act is None:
        return z
    if act == "gelu":
        return _gelu_tanh(z)
    if act == "tanh":
        return jnp.tanh(z)
    if act == "sigmoid":
        return _sigmoid(z)
    if act == "gla_gate":
        return -_softplus(-z) * (1.0 / GLA_TAU)
    if act == "rwkv_decay":
        return -jnp.exp(-_softplus(-z) - 0.5)
    raise ValueError(act)


def _proj_tiles(m, k, n, x_bytes, o_bytes, n_weights):
    best = None
    for tm in (2048, 1536, 1024, 512, 256, 128, 64, 32, 16, 8):
        if m % tm:
            continue
        for tn in (1024, 512, 256, 128, n):
            if n % tn:
                continue
            need = (2 * tm * k * x_bytes + n_weights * k * tn * (2 * 4 + 2)
                    + 2 * tm * tn * o_bytes + (tm * tn * 4 if o_bytes < 4 else 0))
            if need <= PROJ_VMEM_BUDGET and (best is None
                                             or (tm * tn, tn) > (best[0] * best[1], best[1])):
                best = (tm, tn)
    assert best is not None
    return best


def _row_chunks(tm, rows=512):
    rows = min(rows, tm)
    return [slice(r, r + rows) for r in range(0, tm, rows)]


def _mm_kernel(x_ref, w_ref, b_ref, o_ref, wb, *, act):
    @pl.when(pl.program_id(1) == 0)
    def _():
        wb[...] = w_ref[...].astype(BF16)

    for rows in _row_chunks(x_ref.shape[0]):
        z = _dot(x_ref[rows, :], wb[...]) + b_ref[...]
        o_ref[rows, :] = _apply_act(z, act).astype(o_ref.dtype)


def _mm(x, w, bias=None, act=None, out_dtype=F32, col0=0, ncols=None, widx=()):
    m, k = x.shape
    n_total = w.shape[-1]
    ncols = n_total if ncols is None else ncols
    tm, tn = _proj_tiles(m, k, ncols, x.dtype.itemsize, jnp.dtype(out_dtype).itemsize, 1)
    assert col0 % tn == 0 and w.ndim == 2 + len(widx)
    off = col0 // tn
    if bias is None:
        bias = jnp.zeros((ncols,), F32)
    return pl.pallas_call(
        functools.partial(_mm_kernel, act=act),
        grid=(ncols // tn, m // tm),
        in_specs=[pl.BlockSpec((tm, k), lambda j, i: (i, 0)),
                  pl.BlockSpec((None,) * len(widx) + (k, tn), lambda j, i: widx + (0, off + j)),
                  pl.BlockSpec((1, tn), lambda j, i: (0, j))],
        out_specs=pl.BlockSpec((tm, tn), lambda j, i: (i, j)),
        out_shape=jax.ShapeDtypeStruct((m, ncols), out_dtype),
        scratch_shapes=[pltpu.VMEM((k, tn), BF16)],
        compiler_params=_cparams("parallel", "arbitrary"),
        name="proj",
    )(x, w, bias.reshape(1, ncols).astype(F32))


def _ffn_in_kernel(x_ref, wg_ref, wu_ref, o_ref, wgb, wub):
    @pl.when(pl.program_id(1) == 0)
    def _():
        wgb[...] = wg_ref[...].astype(BF16)
        wub[...] = wu_ref[...].astype(BF16)

    for rows in _row_chunks(x_ref.shape[0]):
        x = x_ref[rows, :]
        gate = _dot(x, wgb[...])
        up = _dot(x, wub[...])
        o_ref[rows, :] = (gate * _sigmoid(gate) * up).astype(o_ref.dtype)


def _ffn_in(x, w_in, layer):
    m, k = x.shape
    d_ff = w_in.shape[-1] // 2
    tm, tn = _proj_tiles(m, k, d_ff, x.dtype.itemsize, 2, 2)
    off = d_ff // tn
    return pl.pallas_call(
        _ffn_in_kernel,
        grid=(d_ff // tn, m // tm),
        in_specs=[pl.BlockSpec((tm, k), lambda j, i: (i, 0)),
                  pl.BlockSpec((None, k, tn), lambda j, i: (layer, 0, j)),
                  pl.BlockSpec((None, k, tn), lambda j, i: (layer, 0, off + j))],
        out_specs=pl.BlockSpec((tm, tn), lambda j, i: (i, j)),
        out_shape=jax.ShapeDtypeStruct((m, d_ff), BF16),
        scratch_shapes=[pltpu.VMEM((k, tn), BF16)] * 2,
        compiler_params=_cparams("parallel", "arbitrary"),
        name="ffn_in",
    )(x, w_in, w_in)


def _scan_tile(step, n_lat, n_all, reverse):
    if reverse:
        return n_all - 1 - step
    n_ctx = n_all - n_lat
    return jnp.where(step < n_ctx, n_lat + step, step - n_ctx)


def _lru_conv_kernel(x_ref, p_ref, n_ref, w_ref, b_ref, o_ref, buf, *, n_lat, n_all):
    t = pl.program_id(1)
    tt = x_ref.shape[0]
    first = jnp.logical_or(t == 0, t == n_lat)
    last = jnp.logical_or(t == n_lat - 1, t == n_all - 1)
    buf[0:SUBLANES, :] = jnp.where(first, 0.0, p_ref[...])
    buf[SUBLANES:SUBLANES + tt, :] = x_ref[...]
    buf[SUBLANES + tt:2 * SUBLANES + tt, :] = jnp.where(last, 0.0, n_ref[...])
    acc = b_ref[...] + w_ref[0:1, :] * buf[SUBLANES - 2:SUBLANES - 2 + tt, :]
    acc = acc + w_ref[1:2, :] * buf[SUBLANES - 1:SUBLANES - 1 + tt, :]
    acc = acc + w_ref[2:3, :] * buf[SUBLANES:SUBLANES + tt, :]
    acc = acc + w_ref[3:4, :] * buf[SUBLANES + 1:SUBLANES + 1 + tt, :]
    o_ref[...] = acc


def _lru_conv(rec, conv_w, conv_b, n_seq):
    b, n_tot, w = rec.shape
    tt = _pick(n_tot - n_seq, (256, 128, 64, 32, 16, 8))
    assert n_seq % tt == 0 and conv_w.shape[0] == 4
    n_lat, n_all = n_seq // tt, n_tot // tt
    r8 = tt // SUBLANES
    last8 = n_tot // SUBLANES - 1
    return pl.pallas_call(
        functools.partial(_lru_conv_kernel, n_lat=n_lat, n_all=n_all),
        grid=(b, n_all),
        in_specs=[pl.BlockSpec((None, tt, w), lambda i, j: (i, j, 0)),
                  pl.BlockSpec((None, SUBLANES, w), lambda i, j: (i, jnp.maximum(j * r8 - 1, 0), 0)),
                  pl.BlockSpec((None, SUBLANES, w), lambda i, j: (i, jnp.minimum((j + 1) * r8, last8), 0)),
                  pl.BlockSpec((4, w), lambda i, j: (0, 0)),
                  pl.BlockSpec((1, w), lambda i, j: (0, 0))],
        out_specs=pl.BlockSpec((None, tt, w), lambda i, j: (i, j, 0)),
        out_shape=jax.ShapeDtypeStruct((b, n_tot, w), F32),
        scratch_shapes=[pltpu.VMEM((tt + 2 * SUBLANES, w), F32)],
        compiler_params=_cparams("parallel", "parallel"),
        name="lru_conv",
    )(rec, rec, rec, conv_w, conv_b.reshape(1, w))


def _lru_scan_kernel(*refs, reverse, combine, blk):
    if combine:
        x_ref, wc_ref, gb_ref, ll_ref, hf_ref, gate_ref, o_ref, a_s, b_s, carry = refs
    else:
        x_ref, wc_ref, gb_ref, ll_ref, o_ref, a_s, b_s, carry = refs
    tt, w = x_ref.shape
    nlb = w // LANES

    @pl.when(pl.program_id(1) == 0)
    def _():
        carry[...] = jnp.zeros_like(carry)

    def put(dst, n, val):
        if blk >= LANES:
            for p in range(blk // LANES):
                dst[n * (blk // LANES) + p] = val[:, p * LANES:(p + 1) * LANES]
        else:
            off = (n * blk) % LANES
            dst[(n * blk) // LANES, :, off:off + blk] = val

    sp = _softplus(-ll_ref[...])
    for n in range(w // blk):
        sl = slice(n * blk, (n + 1) * blk)
        xb = x_ref[:, sl]
        gates = _dot(xb, wc_ref[n])
        rr = _sigmoid(gates[:, :blk] + gb_ref[0:1, sl])
        ii = _sigmoid(gates[:, blk:] + gb_ref[1:2, sl])
        a = jnp.exp(-LRU_C * rr * sp[:, sl])
        put(a_s, n, a)
        om = 1.0 - a * a
        put(b_s, n, om * lax.rsqrt(jnp.maximum(om, 1e-30)) * (ii * xb))

    groups = tt // SUBLANES
    row = lax.broadcasted_iota(jnp.int32, (groups, SUBLANES, LANES), 1)
    for n in range(nlb):
        a = a_s[n].reshape(groups, SUBLANES, LANES)
        bt = b_s[n].reshape(groups, SUBLANES, LANES)
        for s in (1, 2, 4):
            keep = (row < SUBLANES - s) if reverse else (row >= s)
            shift = SUBLANES - s if reverse else s
            a_sh = jnp.where(keep, pltpu.roll(a, shift, 1), 1.0)
            b_sh = jnp.where(keep, pltpu.roll(bt, shift, 1), 0.0)
            bt = a * b_sh + bt
            a = a * a_sh
        c = carry[n]
        outs = [None] * groups
        for g in (range(groups - 1, -1, -1) if reverse else range(groups)):
            h = a[g] * c + bt[g]
            outs[g] = h
            c = h[0:1, :] if reverse else h[SUBLANES - 1:SUBLANES, :]
        carry[n] = c
        hl = jnp.concatenate(outs, axis=0)
        sl = slice(n * LANES, (n + 1) * LANES)
        if combine:
            o_ref[:, sl] = ((hl + hf_ref[:, sl]) * gate_ref[:, sl]).astype(o_ref.dtype)
        else:
            o_ref[:, sl] = hl


def _lru_scan(x, wcat, gate_b, log_lambda, n_seq, reverse, h_fwd=None, gate=None):
    b, n_tot, w = x.shape
    blk = w // LRU_BLOCKS
    tt = _pick(n_tot - n_seq, (256, 128, 64))
    assert n_seq % tt == 0 and tt % (SUBLANES * SUBLANES) == 0
    n_lat, n_all = n_seq // tt, n_tot // tt
    combine = h_fwd is not None
    tile = lambda i, j: (i, _scan_tile(j, n_lat, n_all, reverse), 0)
    in_specs = [pl.BlockSpec((None, tt, w), tile),
                pl.BlockSpec((LRU_BLOCKS, blk, 2 * blk), lambda i, j: (0, 0, 0)),
                pl.BlockSpec((2, w), lambda i, j: (0, 0)),
                pl.BlockSpec((1, w), lambda i, j: (0, 0))]
    args = [x, wcat, gate_b, log_lambda.reshape(1, w)]
    if combine:
        in_specs += [pl.BlockSpec((None, tt, w), tile), pl.BlockSpec((None, tt, w), tile)]
        args += [h_fwd, gate]
    return pl.pallas_call(
        functools.partial(_lru_scan_kernel, reverse=reverse, combine=combine, blk=blk),
        grid=(b, n_all),
        in_specs=in_specs,
        out_specs=pl.BlockSpec((None, tt, w), tile),
        out_shape=jax.ShapeDtypeStruct((b, n_tot, w), BF16 if combine else F32),
        scratch_shapes=([pltpu.VMEM((w // LANES, tt, LANES), F32)] * 2
                        + [pltpu.VMEM((w // LANES, 1, LANES), F32)]),
        compiler_params=_cparams("parallel", "arbitrary"),
        name="lru_scan_bwd" if reverse else "lru_scan_fwd",
    )(*args)


def _rglru_mixer(us, n_seq, j, w_in, b_in, conv_w, conv_b, gate_w, gate_b, log_lambda, w_out, b_out):
    b, n_tot, d = us.shape
    w = w_in.shape[-1] // 2
    x2 = us.reshape(b * n_tot, d)
    gate = _mm(x2, w_in, b_in[:w], act="gelu", col0=0, ncols=w, widx=(j,)).reshape(b, n_tot, w)
    rec = _mm(x2, w_in, b_in[w:], col0=w, ncols=w, widx=(j,)).reshape(b, n_tot, w)
    x = _lru_conv(rec, conv_w, conv_b, n_seq)
    wcat = jnp.concatenate([gate_w[:, 0], gate_w[:, 1]], axis=-1).astype(BF16)
    h_f = _lru_scan(x, wcat[0], gate_b[0], log_lambda[0], n_seq, False)
    hy = _lru_scan(x, wcat[1], gate_b[1], log_lambda[1], n_seq, True, h_fwd=h_f, gate=gate)
    y = _mm(hy.reshape(b * n_tot, w), w_out, b_out, widx=(j,))
    return y.reshape(b, n_tot, d)


def _gla_kernel(*refs, reverse, combine, scale):
    if combine:
        q_ref, k_ref, v_ref, g_ref, of_ref, r_ref, br_ref, ng_ref, o_ref, st = refs
    else:
        q_ref, k_ref, v_ref, g_ref, o_ref, st = refs
    tt = q_ref.shape[0]
    c = GLA_CHUNK

    @pl.when(pl.program_id(2) == 0)
    def _():
        st[...] = jnp.zeros_like(st)

    row = lax.broadcasted_iota(jnp.int32, (c, c), 0)
    col = lax.broadcasted_iota(jnp.int32, (c, c), 1)
    tri = (row <= col) if reverse else (row >= col)
    tri_b = tri.astype(BF16)
    n_chunks = tt // c
    n_heads, hv, hk = st.shape
    work = []
    for p in range(n_heads):
        lk = slice(p * hk, (p + 1) * hk)
        lv = slice(p * hv, (p + 1) * hv)
        for ci in (range(n_chunks - 1, -1, -1) if reverse else range(n_chunks)):
            sl = slice(ci * c, (ci + 1) * c)
            k = k_ref[sl, lk]
            v = v_ref[sl, lv].astype(BF16)
            bcum = _dot_parts([tri_b], _split(g_ref[sl, lk], 3), 3)
            b_last = bcum[0:1, :] if reverse else bcum[c - 1:c, :]
            q_e = (q_ref[sl, lk] * scale * jnp.exp(bcum)).astype(BF16)
            k_e = k * jnp.exp(-bcum)
            att = jnp.where(tri, _dot_nt(q_e, k_e), 0.0)
            o = _dot(att, v)
            if combine:
                o = o + of_ref[sl, lv]
            kv = _dot_tn(v, k * jnp.exp(b_last - bcum))
            work.append((sl, lv, q_e, o, jnp.exp(b_last), kv))
    for p in range(n_heads):
        s_t = st[p]
        for sl, lv, q_e, o, decay, kv in work[p * n_chunks:(p + 1) * n_chunks]:
            o = o + _dot_nt(q_e, s_t)
            if combine:
                o = o * lax.rsqrt(jnp.mean(o * o, axis=-1, keepdims=True) + NORM_EPS) * ng_ref[:, lv]
                r = r_ref[sl, lv] + br_ref[:, lv]
                o = o * (r * _sigmoid(r))
            o_ref[sl, lv] = o.astype(o_ref.dtype)
            s_t = s_t * decay + kv
        st[p] = s_t


def _gla_scan(qkvr, g, n_seq, reverse, o_fwd=None, b_r=None, norm_g=None):
    b, n_tot, _ = qkvr.shape
    dk = g.shape[-1]
    hk = dk // GLA_HEADS
    hv = 2 * hk
    tt = _pick(n_tot - n_seq, (256, 128, 64))
    assert n_seq % tt == 0
    n_lat, n_all = n_seq // tt, n_tot // tt
    combine = o_fwd is not None
    hps = 2
    wk, wv = hps * hk, hps * hv
    kq, kv = dk // wk, 2 * dk // wv
    tile = lambda i, h, j: _scan_tile(j, n_lat, n_all, reverse)
    in_specs = [pl.BlockSpec((None, tt, wk), lambda i, h, j: (i, tile(i, h, j), h)),
                pl.BlockSpec((None, tt, wk), lambda i, h, j: (i, tile(i, h, j), kq + h)),
                pl.BlockSpec((None, tt, wv), lambda i, h, j: (i, tile(i, h, j), kv + h)),
                pl.BlockSpec((None, tt, wk), lambda i, h, j: (i, tile(i, h, j), h))]
    args = [qkvr, qkvr, qkvr, g]
    o_spec = pl.BlockSpec((None, tt, wv), lambda i, h, j: (i, tile(i, h, j), h))
    if combine:
        kr = 4 * dk // wv
        vec = pl.BlockSpec((1, wv), lambda i, h, j: (0, h))
        in_specs += [o_spec, pl.BlockSpec((None, tt, wv), lambda i, h, j: (i, tile(i, h, j), kr + h)),
                     vec, vec]
        args += [o_fwd, qkvr, b_r.reshape(1, 2 * dk), norm_g.reshape(1, 2 * dk)]
    return pl.pallas_call(
        functools.partial(_gla_kernel, reverse=reverse, combine=combine, scale=float(hk) ** -0.5),
        grid=(b, GLA_HEADS // hps, n_all),
        in_specs=in_specs,
        out_specs=o_spec,
        out_shape=jax.ShapeDtypeStruct((b, n_tot, 2 * dk), BF16 if combine else F32),
        scratch_shapes=[pltpu.VMEM((hps, hv, hk), F32)],
        compiler_params=_cparams("parallel", "parallel", "arbitrary"),
        name="gla_bwd" if reverse else "gla_fwd",
    )(*args)


def _gla_mixer(us, n_seq, j, w_in, b_r, gate_w1, gate_w2, gate_b, norm_g, w_out):
    b, n_tot, d = us.shape
    m = b * n_tot
    x2 = us.reshape(m, d)
    dk = gate_w2.shape[-1]
    qkvr = _mm(x2, w_in, widx=(j,))
    low = _mm(x2, jnp.concatenate([gate_w1[0], gate_w1[1]], axis=-1))
    zeros = jnp.zeros_like(gate_w2[0])
    o = None
    for dr in range(2):
        w2 = jnp.concatenate([gate_w2[0], zeros] if dr == 0 else [zeros, gate_w2[1]], axis=0)
        g = _mm(low, w2, gate_b[dr], act="gla_gate")
        o = _gla_scan(qkvr.reshape(b, n_tot, -1), g.reshape(b, n_tot, dk), n_seq, dr == 1,
                      o_fwd=o, b_r=b_r, norm_g=norm_g)
    return _mm(o.reshape(m, 2 * dk), w_out, widx=(j,)).reshape(b, n_tot, d)


def _rwkv_mix_kernel(x_ref, p_ref, n_ref, mu_ref, *o_refs, n_lat, n_all):
    t = pl.program_id(1)
    tt = x_ref.shape[0]
    first = jnp.logical_or(t == 0, t == n_lat)
    last = jnp.logical_or(t == n_lat - 1, t == n_all - 1)
    x = x_ref[...]
    prev_row = jnp.where(first, 0.0, p_ref[SUBLANES - 1:SUBLANES, :])
    next_row = jnp.where(last, 0.0, n_ref[0:1, :])
    rid = lax.broadcasted_iota(jnp.int32, x.shape, 0)
    up = jnp.where(rid == 0, prev_row, pltpu.roll(x, 1, 0))
    dn = jnp.where(rid == tt - 1, next_row, pltpu.roll(x, tt - 1, 0))
    dx = 0.5 * (up + dn) - x
    for i, o_ref in enumerate(o_refs):
        o_ref[...] = (x + dx * mu_ref[i:i + 1, :]).astype(o_ref.dtype)


def _rwkv_mix(us, mu, n_seq):
    b, n_tot, d = us.shape
    n_mix = mu.shape[0]
    tt = _pick(n_tot - n_seq, (256, 128, 64, 32, 16, 8))
    assert n_seq % tt == 0
    n_lat, n_all = n_seq // tt, n_tot // tt
    r8 = tt // SUBLANES
    last8 = n_tot // SUBLANES - 1
    out = jax.ShapeDtypeStruct((b, n_tot, d), BF16)
    return pl.pallas_call(
        functools.partial(_rwkv_mix_kernel, n_lat=n_lat, n_all=n_all),
        grid=(b, n_all),
        in_specs=[pl.BlockSpec((None, tt, d), lambda i, j: (i, j, 0)),
                  pl.BlockSpec((None, SUBLANES, d), lambda i, j: (i, jnp.maximum(j * r8 - 1, 0), 0)),
                  pl.BlockSpec((None, SUBLANES, d), lambda i, j: (i, jnp.minimum((j + 1) * r8, last8), 0)),
                  pl.BlockSpec((n_mix, d), lambda i, j: (0, 0))],
        out_specs=[pl.BlockSpec((None, tt, d), lambda i, j: (i, j, 0))] * n_mix,
        out_shape=[out] * n_mix,
        compiler_params=_cparams("parallel", "parallel"),
        name="rwkv_mix",
    )(us, us, us, mu)


def _stack_heads(x, head0):
    return jnp.concatenate([jnp.where(head0, x, 0.0), jnp.where(head0, 0.0, x)], axis=0)


def _split(x, parts):
    out = []
    for _ in range(parts - 1):
        hi = x.astype(BF16)
        out.append(hi)
        x = x - hi.astype(F32)
    out.append(x.astype(BF16))
    return out


def _dot_parts(a_parts, b_parts, order):
    acc = None
    for i, ai in enumerate(a_parts):
        for j, bj in enumerate(b_parts):
            if i + j < order:
                t = jnp.dot(ai, bj, preferred_element_type=F32)
                acc = t if acc is None else acc + t
    return acc


def _dot2(a, b):
    return _dot_parts(_split(a, 2), _split(b, 2), 2)


def _heads_mm(a, b, head0, wide):
    bs = _stack_heads(b, head0)
    return _dot2(a, bs) if wide else _dot(a, bs)


def _rwkv_scan_kernel(*refs, reverse, readout):
    if readout:
        (r_ref, k_ref, v_ref, a_ref, w_ref, kk_ref, ka_ref,
         yf_ref, af_ref, g_ref, rk_ref, lw_ref, lb_ref, o_ref, z_ref) = refs
    else:
        r_ref, k_ref, v_ref, a_ref, w_ref, kk_ref, ka_ref, o_ref, z_ref = refs
    tt = r_ref.shape[0]
    c = RWKV_CHUNK
    lanes = 2 * RWKV_N

    @pl.when(pl.program_id(2) == 0)
    def _():
        z_ref[...] = jnp.zeros_like(z_ref)

    lane = lax.broadcasted_iota(jnp.int32, (1, lanes), 1)
    head0 = lane < RWKV_N
    r2 = lax.broadcasted_iota(jnp.int32, (lanes, lanes), 0)
    c2 = lax.broadcasted_iota(jnp.int32, (lanes, lanes), 1)
    same_head = (r2 < RWKV_N) == (c2 < RWKV_N)
    eye2 = (r2 == c2).astype(F32)
    tr = lax.broadcasted_iota(jnp.int32, (c, c), 0)
    tc = lax.broadcasted_iota(jnp.int32, (c, c), 1)
    tri_b = ((tr <= tc) if reverse else (tr >= tc)).astype(BF16)
    pr = lax.broadcasted_iota(jnp.int32, (2 * c, 4 * c), 0)
    pc = lax.broadcasted_iota(jnp.int32, (2 * c, 4 * c), 1) % c
    incl = (pr >= c).astype(jnp.int32)
    pt = pr % c
    pmask = (pt - incl < pc) if reverse else (pt + incl > pc)
    eye_side = (lax.broadcasted_iota(jnp.int32, (c, 2 * c), 0)
                == lax.broadcasted_iota(jnp.int32, (c, 2 * c), 1) % c).astype(F32)
    kk_scale = kk_ref[...]
    ka_scale = ka_ref[...]
    lvl = (lax.broadcasted_iota(jnp.int32, (c, 2 * c), 0)
           ^ (lax.broadcasted_iota(jnp.int32, (c, 2 * c), 1) % c))

    n_chunks = tt // c
    n_pairs = r_ref.shape[1] // lanes
    order = list(range(n_chunks - 1, -1, -1) if reverse else range(n_chunks))
    sls = [(slice(ci * c, (ci + 1) * c), slice(p * lanes, (p + 1) * lanes))
           for p in range(n_pairs) for ci in order]
    vs = [v_ref[sl] for sl in sls]
    pre = []
    for sl in sls:
        k = k_ref[sl]
        asig = a_ref[sl]
        ld = w_ref[sl]
        kk = k * kk_scale[:, sl[1]]
        kk2 = kk * kk
        sq = jnp.where(head0,
                       jnp.sum(jnp.where(head0, kk2, 0.0), axis=-1, keepdims=True),
                       jnp.sum(jnp.where(head0, 0.0, kk2), axis=-1, keepdims=True))
        kk = kk * lax.rsqrt(jnp.maximum(sq, 1e-24))
        bvec = kk * asig
        kd = k * (1.0 + (asig - 1.0) * ka_scale[:, sl[1]])
        cum = _dot_parts([tri_b], _split(ld, 2), 2)
        tot = cum[0:1, :] if reverse else cum[c - 1:c, :]
        e_neg = jnp.exp(-cum)
        e_rem = jnp.exp(tot - cum)
        pre.append(dict(at=-kk * jnp.exp(cum - ld), rt=r_ref[sl] * jnp.exp(cum),
                        bt=bvec * e_neg, kt=kd * e_neg, bc=bvec * e_rem, kc=kd * e_rem,
                        wc=jnp.exp(tot)))
    ps = []
    for q in pre:
        lhs = jnp.concatenate([q["at"], q["rt"]], axis=0)
        rhs = jnp.concatenate([_stack_heads(q["bt"], head0), _stack_heads(q["kt"], head0)], axis=0)
        ps.append(jnp.where(pmask, _dot_nt(lhs, rhs), 0.0))
    a_ab = [p[:c, :2 * c] for p in ps]
    a_ak = [p[:c, 2 * c:] for p in ps]
    p_rb = [p[c:, :2 * c] for p in ps]
    p_rk = [p[c:, 2 * c:] for p in ps]
    base = [jnp.where((lvl >> 3) == 0, a, 0.0) for a in a_ab]
    pw2 = [_heads_mm(x, x, head0, False) for x in base]
    pw4 = [_heads_mm(x, x, head0, False) for x in pw2]
    inv = [eye_side + x for x in base]
    inv = [d + _heads_mm(d, x, head0, False) for d, x in zip(inv, pw2)]
    inv = [d + _heads_mm(d, x, head0, False) for d, x in zip(inv, pw4)]
    for bit in (3, 4, 5):
        off = [jnp.where((lvl >> bit) == 1, a, 0.0) for a in a_ab]
        ed = [_heads_mm(e, d, head0, False) for e, d in zip(off, inv)]
        inv = [d + _heads_mm(d, x, head0, False) for d, x in zip(inv, ed)]
    res = [eye_side - d + _heads_mm(a, d, head0, True) for a, d in zip(a_ab, inv)]
    inv = [d + _heads_mm(d, x, head0, False) for d, x in zip(inv, res)]
    at2 = [_heads_mm(d, q["at"], head0, True) for d, q in zip(inv, pre)]
    av = [_heads_mm(a, v, head0, False) for a, v in zip(a_ak, vs)]
    u_loc = [_heads_mm(d, x, head0, True) for d, x in zip(inv, av)]
    r_eff = [q["rt"] + _heads_mm(p, x, head0, False) for q, p, x in zip(pre, p_rb, at2)]
    y_loc = [_heads_mm(pb, u, head0, False) + _heads_mm(pk, v, head0, False)
             for pb, u, pk, v in zip(p_rb, u_loc, p_rk, vs)]
    m_t = [eye2 * q["wc"] + jnp.where(same_head, _dot_tn(q["bc"], x), 0.0)
           for q, x in zip(pre, at2)]
    n_t = [jnp.where(same_head, _dot_tn(q["bc"], u) + _dot_tn(q["kc"], v), 0.0)
           for q, u, v in zip(pre, u_loc, vs)]
    def head_sum(t):
        return jnp.where(head0,
                         jnp.sum(jnp.where(head0, t, 0.0), axis=-1, keepdims=True),
                         jnp.sum(jnp.where(head0, 0.0, t), axis=-1, keepdims=True))

    def read_out(sl, y):
        y = y + yf_ref[sl]
        yc = y - head_sum(y) * (1.0 / RWKV_N)
        var = head_sum(yc * yc) * (1.0 / RWKV_N)
        y = yc * lax.rsqrt(var + RWKV_GN_EPS) * lw_ref[:, sl[1]] + lb_ref[:, sl[1]]
        k = k_ref[sl]
        ka = ka_scale[:, sl[1]]
        k_both = k * (1.0 + (af_ref[sl] - 1.0) * ka) + k * (1.0 + (a_ref[sl] - 1.0) * ka)
        bonus = head_sum(r_ref[sl] * k_both * rk_ref[:, sl[1]]) * v_ref[sl]
        return (y + bonus) * g_ref[sl]

    for p in range(n_pairs):
        z = z_ref[p]
        for i in range(p * n_chunks, (p + 1) * n_chunks):
            y = _dot(r_eff[i], z) + y_loc[i]
            o_ref[sls[i]] = read_out(sls[i], y).astype(o_ref.dtype) if readout else y
            z = _dot2(m_t[i], z) + n_t[i]
        z_ref[p] = z


def _rwkv_scan(r, k, v, asig, logw, k_k, k_a, n_seq, reverse, readout=None):
    b, n_tot, d = r.shape
    lanes = 2 * RWKV_N
    tt = _pick(n_tot - n_seq, (256, 128, 64))
    assert n_seq % tt == 0 and d % lanes == 0
    n_lat, n_all = n_seq // tt, n_tot // tt
    n_pairs = _pick(d // lanes, (4, 2, 1))
    wide = n_pairs * lanes
    tile = pl.BlockSpec((None, tt, wide),
                        lambda i, h, j: (i, _scan_tile(j, n_lat, n_all, reverse), h))
    vec = pl.BlockSpec((1, wide), lambda i, h, j: (0, h))
    in_specs = [tile] * 5 + [vec, vec]
    args = [r, k, v, asig, logw, k_k.reshape(1, d), k_a.reshape(1, d)]
    if readout is not None:
        y_other, a_other, gate, r_k, ln_w, ln_b = readout
        in_specs += [tile] * 3 + [vec] * 3
        args += [y_other, a_other, gate, r_k.reshape(1, d), ln_w.reshape(1, d), ln_b.reshape(1, d)]
    return pl.pallas_call(
        functools.partial(_rwkv_scan_kernel, reverse=reverse, readout=readout is not None),
        grid=(b, d // wide, n_all),
        in_specs=in_specs,
        out_specs=tile,
        out_shape=jax.ShapeDtypeStruct((b, n_tot, d), F32 if readout is None else BF16),
        scratch_shapes=[pltpu.VMEM((n_pairs, lanes, lanes), F32)],
        compiler_params=_cparams("parallel", "parallel", "arbitrary"),
        name="rwkv_bwd" if reverse else "rwkv_fwd",
    )(*args)


def _pad_rows(w2, dr):
    zeros = jnp.zeros_like(w2[0])
    return jnp.concatenate([w2[0], zeros] if dr == 0 else [zeros, w2[1]], axis=0)


def _rwkv_mixer(us, n_seq, j, mu, w_rkv, w0, w1, w2, a0, a1, a2, g1, g2, k_k, k_a, r_k,
                ln_w, ln_b, w_out):
    b, n_tot, d = us.shape
    m = b * n_tot
    xr, xw, xk, xv, xa, xg = [t.reshape(m, d) for t in _rwkv_mix(us, mu, n_seq)]
    r = _mm(xr, w_rkv, widx=(j, 0))
    k = _mm(xk, w_rkv, widx=(j, 1))
    v = _mm(xv, w_rkv, widx=(j, 2))
    w_low = _mm(xw, jnp.concatenate([w1[0], w1[1]], axis=-1), act="tanh")
    a_low = _mm(xa, jnp.concatenate([a1[0], a1[1]], axis=-1))
    g_low = _mm(xg, g1, act="sigmoid")
    gate = _mm(g_low, g2)
    sh = (b, n_tot, d)
    r3, k3, v3 = r.reshape(sh), k.reshape(sh), v.reshape(sh)
    logw = [_mm(w_low, _pad_rows(w2, dr), w0[dr], act="rwkv_decay").reshape(sh) for dr in range(2)]
    asig = [_mm(a_low, _pad_rows(a2, dr), a0[dr], act="sigmoid").reshape(sh) for dr in range(2)]
    y_f = _rwkv_scan(r3, k3, v3, asig[0], logw[0], k_k, k_a, n_seq, False)
    hy = _rwkv_scan(r3, k3, v3, asig[1], logw[1], k_k, k_a, n_seq, True,
                    readout=(y_f, asig[0], gate.reshape(sh), r_k.reshape(-1), ln_w, ln_b))
    return _mm(hy.reshape(m, d), w_out, widx=(j,)).reshape(b, n_tot, d)


def kernel(x, c, ctx, c_ctx, ada_w, ada_b, norm_g, ffn_w_in, ffn_w_out, lru_w_in, lru_b_in, lru_conv_w, lru_conv_b, lru_gate_w, lru_gate_b, lru_log_lambda, lru_w_out, lru_b_out, gla_w_in, gla_b_r, gla_gate_w1, gla_gate_w2, gla_gate_b, gla_norm_g, gla_w_out, rwkv_mu, rwkv_w_rkv, rwkv_w0, rwkv_w1, rwkv_w2, rwkv_a0, rwkv_a1, rwkv_a2, rwkv_g1, rwkv_g2, rwkv_k_k, rwkv_k_a, rwkv_r_k, rwkv_ln_w, rwkv_ln_b, rwkv_w_out):
    b, n_seq, d = x.shape
    depth = ada_w.shape[0]
    assert b < 8
    cond = jnp.concatenate([c, c_ctx[None, :], jnp.zeros((8 - b - 1, d), F32)], axis=0)
    mods = _mods(cond, ada_w, ada_b).reshape(depth, 8, N_MOD, d)
    h_lat, h_ctx = x, ctx
    for i in range(depth):
        ctx_out = i < depth - 1
        kind, j = i % N_MIXERS, i // N_MIXERS
        col_major = i % 2 == 1
        m_lat, m_ctx = mods[i, :b], mods[i, b]
        us = _norm_mod(h_lat, h_ctx, norm_g[i, 0], m_lat, m_ctx, 0, 1, col_major,
                       F32 if kind == 2 else BF16)
        if kind == 0:
            y = _rglru_mixer(us, n_seq, j, lru_w_in, lru_b_in[j], lru_conv_w[j], lru_conv_b[j],
                             lru_gate_w[j], lru_gate_b[j], lru_log_lambda[j], lru_w_out,
                             lru_b_out[j])
        elif kind == 1:
            y = _gla_mixer(us, n_seq, j, gla_w_in, gla_b_r[j], gla_gate_w1[j], gla_gate_w2[j],
                           gla_gate_b[j], gla_norm_g[j], gla_w_out)
        else:
            y = _rwkv_mixer(us, n_seq, j, rwkv_mu[j], rwkv_w_rkv, rwkv_w0[j], rwkv_w1[j],
                            rwkv_w2[j], rwkv_a0[j], rwkv_a1[j], rwkv_a2[j], rwkv_g1[j],
                            rwkv_g2[j], rwkv_k_k[j], rwkv_k_a[j], rwkv_r_k[j], rwkv_ln_w[j],
                            rwkv_ln_b[j], rwkv_w_out)
        h_lat, h_ctx, us = _resid(h_lat, h_ctx if ctx_out else None, y, norm_g[i, 1], m_lat,
                                  m_ctx, 2, col_major, follow=(3, 4), g2=norm_g[i, 2])
        n_tot = us.shape[1]
        mid = _ffn_in(us.reshape(b * n_tot, d), ffn_w_in, i)
        y = _mm(mid, ffn_w_out, widx=(i,)).reshape(b, n_tot, d)
        h_lat, h_ctx, _ = _resid(h_lat, h_ctx, y, norm_g[i, 3], m_lat, m_ctx, 5, col_major)
    return h_lat
```
